```python
import jax, jax.numpy as jnp
from jax import lax
import numpy as np

D_MODEL = 4096
BATCH = 4
SEQ = 2048
DEPTH = 1
DEC_BATCH = 32
DEC_SEQ = 1
PAST_LEN = 8192
PAGE_SIZE = 128

HEAD_DIM = 128
CONV_CH = D_MODEL // 2
ATTN_WIDTH = D_MODEL - CONV_CH
N_HEADS = ATTN_WIDTH // HEAD_DIM
D_IN = 2 * CONV_CH + 3 * ATTN_WIDTH
CONV_KERNEL = 31
CONV_STATE = CONV_KERNEL - 1
MOBA_BLOCK = 256
MOBA_TOPK = 3
MOBA_QUERY_ROWS = 64
ROPE_THETA = 10000.0
D_FF = 4 * D_MODEL
PLE_DIM = 256
NORM_EPS = 1e-6

kernel_name = 'hymba_conformer_moba_decoder_step'


def rms_norm(x, g):
    xf = x.astype(jnp.float32)
    y = xf * lax.rsqrt(jnp.mean(xf * xf, axis=-1, keepdims=True) + NORM_EPS)
    return (y * g.astype(jnp.float32)).astype(x.dtype)


def layer_norm(x, g, b):
    xf = x.astype(jnp.float32)
    mu = jnp.mean(xf, axis=-1, keepdims=True)
    xc = xf - mu
    var = jnp.mean(xc * xc, axis=-1, keepdims=True)
    y = xc * lax.rsqrt(var + NORM_EPS) * g.astype(jnp.float32) + b.astype(jnp.float32)
    return y.astype(x.dtype)


def rope(x, pos0):
    s = x.shape[1]
    inv = 1.0 / (ROPE_THETA ** (jnp.arange(0, HEAD_DIM, 2, dtype=jnp.float32) / HEAD_DIM))
    pos = (jnp.arange(s, dtype=jnp.int32) + pos0).astype(jnp.float32)
    ang = pos[:, None] * inv[None, :]
    cos = jnp.cos(ang)[None, :, None, :]
    sin = jnp.sin(ang)[None, :, None, :]
    xf = x.astype(jnp.float32)
    x1, x2 = xf[..., : HEAD_DIM // 2], xf[..., HEAD_DIM // 2:]
    out = jnp.concatenate([x1 * cos - x2 * sin, x2 * cos + x1 * sin], axis=-1)
    return out.astype(x.dtype)


def conformer_conv(a, gate, buf, w_dw, b_dw, ln_g, ln_b):
    u = a * jax.nn.sigmoid(gate)
    ext = jnp.concatenate([buf.astype(u.dtype), u], axis=1)
    y = lax.conv_general_dilated(ext, w_dw[:, None, :].astype(u.dtype), window_strides=(1,),
                                 padding='VALID', dimension_numbers=('NWC', 'WIO', 'NWC'),
                                 feature_group_count=CONV_CH) + b_dw
    y = jax.nn.silu(layer_norm(y, ln_g, ln_b))
    return y, ext[:, -CONV_STATE:]


def moba_attention(q, k, v, q_pos0):
    B, Sq, H, Dh = q.shape
    L = k.shape[1]
    nb = -(-L // MOBA_BLOCK)
    pad = nb * MOBA_BLOCK - L
    k_blk = jnp.pad(k, ((0, 0), (0, pad), (0, 0), (0, 0))).reshape(B, nb, MOBA_BLOCK, H, Dh)
    v_blk = jnp.pad(v, ((0, 0), (0, pad), (0, 0), (0, 0))).reshape(B, nb, MOBA_BLOCK, H, Dh)
    k_mean = jnp.mean(k_blk.astype(jnp.float32), axis=2)
    n_sel = min(MOBA_TOPK, nb)
    qc = max(1, min(Sq, MOBA_QUERY_ROWS // B))
    n_chunks = -(-Sq // qc)
    q_pad = jnp.pad(q, ((0, 0), (0, n_chunks * qc - Sq), (0, 0), (0, 0)))
    q_chunks = q_pad.reshape(B, n_chunks, qc, H, Dh).swapaxes(0, 1)
    starts = q_pos0 + jnp.arange(n_chunks, dtype=jnp.int32) * qc
    b_idx = jnp.arange(B)[:, None, None, None]
    h_idx = jnp.arange(H)[None, None, :, None]
    offs = jnp.arange(MOBA_BLOCK, dtype=jnp.int32)
    blk_ids = jnp.arange(nb, dtype=jnp.int32)
    scale = Dh ** -0.5

    def one_chunk(args):
        qb, start = args
        t = start + jnp.arange(qc, dtype=jnp.int32)
        bq = t // MOBA_BLOCK
        bq4 = bq[None, :, None, None]
        qf = qb.astype(jnp.float32)
        gate = jnp.einsum('bqhd,bnhd->bqhn', qf, k_mean)
        gate = jnp.where(blk_ids[None, None, None, :] < bq4, gate, -jnp.inf)
        _, sel = lax.top_k(gate, n_sel)
        sel = sel.astype(jnp.int32)
        own = jnp.broadcast_to(jnp.minimum(bq, nb - 1)[None, :, None, None], (B, qc, H, 1))
        idx = jnp.concatenate([sel, own], axis=-1)
        ok_blk = jnp.concatenate([sel < bq4, jnp.ones((B, qc, H, 1), dtype=bool)], axis=-1)
        k_sel = k_blk[b_idx, idx, :, h_idx, :]
        v_sel = v_blk[b_idx, idx, :, h_idx, :]
        kpos = idx[..., None] * MOBA_BLOCK + offs
        mask = ok_blk[..., None] & (kpos <= t[None, :, None, None, None])
        s = jnp.einsum('bqhd,bqhnkd->bqhnk', qb, k_sel, preferred_element_type=jnp.float32) * scale
        s = jnp.where(mask, s, -jnp.inf).reshape(B, qc, H, -1)
        pr = jax.nn.softmax(s, axis=-1).reshape(mask.shape)
        return jnp.einsum('bqhnk,bqhnkd->bqhd', pr.astype(v_sel.dtype), v_sel)

    out = lax.map(one_chunk, (q_chunks, starts))
    return out.swapaxes(0, 1).reshape(B, n_chunks * qc, H, Dh)[:, :Sq]


def decoder_layer(x, p, past_k, past_v, conv_buf, pos0, g_mix, w_in, w_dw, b_dw, g_conv_ln,
                  b_conv_ln, w_out, g_ffn, w_up, w_down, g_ple, w_ple_gate, w_ple_proj):
    B, S, _ = x.shape
    z = rms_norm(x, g_mix)
    proj = z @ w_in
    a, gt, q, k, v = jnp.split(proj, [CONV_CH, 2 * CONV_CH, 2 * CONV_CH + ATTN_WIDTH,
                                      2 * CONV_CH + 2 * ATTN_WIDTH], axis=-1)
    conv_out, new_buf = conformer_conv(a, gt, conv_buf, w_dw, b_dw, g_conv_ln, b_conv_ln)
    q = rope(q.reshape(B, S, N_HEADS, HEAD_DIM), pos0)
    k = rope(k.reshape(B, S, N_HEADS, HEAD_DIM), pos0)
    v = v.reshape(B, S, N_HEADS, HEAD_DIM)
    if past_k is None:
        k_all, v_all = k, v
    else:
        k_all = jnp.concatenate([past_k.astype(k.dtype), k], axis=1)
        v_all = jnp.concatenate([past_v.astype(v.dtype), v], axis=1)
    attn = moba_attention(q, k_all, v_all, pos0).reshape(B, S, ATTN_WIDTH)
    h = x + jnp.concatenate([conv_out, attn], axis=-1) @ w_out
    h = h + jnp.square(jax.nn.relu(rms_norm(h, g_ffn) @ w_up)) @ w_down
    h = h + jax.nn.sigmoid(rms_norm(h, g_ple) @ w_ple_gate) * (p @ w_ple_proj)
    return h, k, v, new_buf


def setup_inputs(seed: int = 0) -> dict:
    key = jax.random.key(seed)
    ks = jax.random.split(key, 24)
    f32 = jnp.float32
    n_pages = PAST_LEN // PAGE_SIZE
    n_pool = (5 * DEC_BATCH * n_pages) // 4

    def nrm(k, shape, scale):
        return jax.random.normal(k, shape, f32) * scale

    page_table = jax.random.permutation(ks[5], n_pool)[: DEC_BATCH * n_pages]
    page_table = page_table.reshape(DEC_BATCH, n_pages).astype(jnp.int32)
    return {
        'x_prompt': nrm(ks[0], (BATCH, SEQ, D_MODEL), 1.0),
        'x_sample': nrm(ks[1], (DEC_BATCH, DEC_SEQ, D_MODEL), 1.0),
        'cache_k': nrm(ks[2], (DEPTH, n_pool, PAGE_SIZE, N_HEADS, HEAD_DIM), 1.0),
        'cache_v': nrm(ks[3], (DEPTH, n_pool, PAGE_SIZE, N_HEADS, HEAD_DIM), 1.0),
        'state_conv': nrm(ks[4], (DEPTH, DEC_BATCH, CONV_STATE, CONV_CH), 1.0),
        'page_table': page_table,
        'p_prompt': nrm(ks[6], (DEPTH, BATCH, SEQ, PLE_DIM), 1.0),
        'p_sample': nrm(ks[7], (DEPTH, DEC_BATCH, DEC_SEQ, PLE_DIM), 1.0),
        'g_mix': 1.0 + nrm(ks[8], (DEPTH, D_MODEL), 0.01),
        'w_in': nrm(ks[9], (DEPTH, D_MODEL, D_IN), D_MODEL ** -0.5),
        'w_dw': nrm(ks[10], (DEPTH, CONV_KERNEL, CONV_CH), CONV_KERNEL ** -0.5),
        'b_dw': nrm(ks[11], (DEPTH, CONV_CH), 0.01),
        'g_conv_ln': 1.0 + nrm(ks[12], (DEPTH, CONV_CH), 0.01),
        'b_conv_ln': nrm(ks[13], (DEPTH, CONV_CH), 0.01),
        'w_out': nrm(ks[14], (DEPTH, CONV_CH + ATTN_WIDTH, D_MODEL), (CONV_CH + ATTN_WIDTH) ** -0.5),
        'g_ffn': 1.0 + nrm(ks[15], (DEPTH, D_MODEL), 0.01),
        'w_up': nrm(ks[16], (DEPTH, D_MODEL, D_FF), D_MODEL ** -0.5),
        'w_down': nrm(ks[17], (DEPTH, D_FF, D_MODEL), D_FF ** -0.5),
        'g_ple': 1.0 + nrm(ks[18], (DEPTH, D_MODEL), 0.01),
        'w_ple_gate': nrm(ks[19], (DEPTH, D_MODEL, D_MODEL), D_MODEL ** -0.5),
        'w_ple_proj': nrm(ks[20], (DEPTH, PLE_DIM, D_MODEL), PLE_DIM ** -0.5),
        'g_final': 1.0 + nrm(ks[21], (D_MODEL,), 0.01),
    }


def reference(x_prompt, x_sample, cache_k, cache_v, state_conv, page_table, p_prompt, p_sample,
              g_mix, w_in, w_dw, b_dw, g_conv_ln, b_conv_ln, w_out, g_ffn, w_up, w_down,
              g_ple, w_ple_gate, w_ple_proj, g_final):
    n_pages = page_table.shape[1]
    past_len = n_pages * cache_k.shape[2]
    db = x_sample.shape[0]
    hp, hs = x_prompt, x_sample
    kp_l, vp_l, cp_l, ks_l, vs_l, cs_l = [], [], [], [], [], []
    for l in range(DEPTH):
        lw = (g_mix[l], w_in[l], w_dw[l], b_dw[l], g_conv_ln[l], b_conv_ln[l], w_out[l],
              g_ffn[l], w_up[l], w_down[l], g_ple[l], w_ple_gate[l], w_ple_proj[l])
        zero_buf = jnp.zeros((hp.shape[0], CONV_STATE, CONV_CH), hp.dtype)
        hp, k_new, v_new, buf_new = decoder_layer(hp, p_prompt[l], None, None, zero_buf, 0, *lw)
        kp_l.append(k_new)
        vp_l.append(v_new)
        cp_l.append(buf_new)
        past_k = cache_k[l][page_table].reshape(db, past_len, N_HEADS, HEAD_DIM)
        past_v = cache_v[l][page_table].reshape(db, past_len, N_HEADS, HEAD_DIM)
        hs, ks_new, vs_new, sbuf_new = decoder_layer(hs, p_sample[l], past_k, past_v,
                                                     state_conv[l], past_len, *lw)
        ks_l.append(ks_new)
        vs_l.append(vs_new)
        cs_l.append(sbuf_new)
    y_prompt = rms_norm(hp, g_final)
    y_sample = rms_norm(hs, g_final)
    return (y_prompt, y_sample, jnp.stack(kp_l), jnp.stack(vp_l), jnp.stack(cp_l),
            jnp.stack(ks_l), jnp.stack(vs_l), jnp.stack(cs_l))
```

```python
import functools

import jax
import jax.numpy as jnp
from jax import lax
from jax.experimental import pallas as pl
from jax.experimental.pallas import tpu as pltpu

F32 = jnp.float32
BF16 = jnp.bfloat16

MOBA_BLOCK = 256
MOBA_TOPK = 3
ROPE_THETA = 10000.0
NORM_EPS = 1e-6
LANES = 128
SUBLANES = 8
CONV_HALO = 32
VMEM_LIMIT = 60 * 1024 * 1024


def _params(*sem):
    return pltpu.CompilerParams(dimension_semantics=sem, vmem_limit_bytes=VMEM_LIMIT)


def _tile(n, pref):
    if n <= pref:
        return n
    t = pref
    while n % t:
        t //= 2
    return t


def _rmsnorm_kernel(x_ref, g_ref, o_ref):
    x = x_ref[...]
    ms = jnp.mean(x * x, axis=-1, keepdims=True)
    o_ref[...] = (x * lax.rsqrt(ms + NORM_EPS) * g_ref[...]).astype(o_ref.dtype)


def _rmsnorm(x, g, out_dtype):
    m, d = x.shape
    tm = _tile(m, 256)
    return pl.pallas_call(
        _rmsnorm_kernel,
        grid=(m // tm,),
        in_specs=[pl.BlockSpec((tm, d), lambda i: (i, 0)), pl.BlockSpec((1, d), lambda i: (0, 0))],
        out_specs=pl.BlockSpec((tm, d), lambda i: (i, 0)),
        out_shape=jax.ShapeDtypeStruct((m, d), out_dtype),
        compiler_params=_params("parallel"),
        name="rmsnorm",
    )(x, g.reshape(1, d))


def _mm_kernel(*refs, n_lhs, dots, n_extra, epilogue):
    lhs = refs[:n_lhs]
    ws = refs[n_lhs:n_lhs + len(dots)]
    extras = refs[n_lhs + len(dots):n_lhs + len(dots) + n_extra]
    out = refs[-1]
    accs = {}
    for (li, group), w in zip(dots, ws):
        a = lhs[li][...]
        if a.dtype != BF16:
            a = a.astype(BF16)
        d = jnp.dot(a, w[...], preferred_element_type=F32)
        accs[group] = d if group not in accs else accs[group] + d
    res = epilogue([accs[g] for g in sorted(accs)], [e[...] for e in extras])
    out[...] = res.astype(out.dtype)


def _matmul(lhs, dots, extras, epilogue, n_cols, tn, out_dtype, name, tm_pref=512):
    m = lhs[0].shape[0]
    tm = _tile(m, tm_pref)
    in_specs = [pl.BlockSpec((tm, a.shape[1]), lambda j, i: (i, 0)) for a in lhs]
    operands = list(lhs)
    for li, w, rb, co, _ in dots:
        kl = lhs[li].shape[1]
        assert co % tn == 0
        in_specs.append(pl.BlockSpec((kl, tn), functools.partial(lambda j, i, rb, cb: (rb, cb + j), rb=rb, cb=co // tn)))
        operands.append(w)
    for arr, blk, imap in extras:
        in_specs.append(pl.BlockSpec(blk, imap))
        operands.append(arr)
    kern = functools.partial(_mm_kernel, n_lhs=len(lhs), dots=[(d[0], d[4]) for d in dots],
                             n_extra=len(extras), epilogue=epilogue)
    return pl.pallas_call(
        kern,
        grid=(n_cols // tn, m // tm),
        in_specs=in_specs,
        out_specs=pl.BlockSpec((tm, tn), lambda j, i: (i, j)),
        out_shape=jax.ShapeDtypeStruct((m, n_cols), out_dtype),
        compiler_params=_params("parallel", "parallel"),
        name=name,
    )(*operands)


def _ep_plain(accs, extras):
    return accs[0]


def _ep_glu(accs, extras):
    return accs[0] * jax.nn.sigmoid(accs[1])


def _ep_resid(accs, extras):
    return extras[0] + accs[0]


def _ep_ple(accs, extras):
    return extras[0] + jax.nn.sigmoid(accs[0]) * accs[1]


def _ep_rope(accs, extras):
    cos, sin_signed = extras
    y = accs[0]
    hd = cos.shape[1]
    outs = []
    for h in range(y.shape[1] // hd):
        xh = y[:, h * hd:(h + 1) * hd]
        outs.append(xh * cos + pltpu.roll(xh, hd // 2, 1) * sin_signed)
    return jnp.concatenate(outs, axis=1)


def _rope_tables(pos0, s, hd):
    inv = 1.0 / (ROPE_THETA ** (jnp.arange(0, hd, 2, dtype=F32) / hd))
    pos = (jnp.arange(s, dtype=jnp.int32) + pos0).astype(F32)
    ang = pos[:, None] * inv[None, :]
    cos, sin = jnp.cos(ang), jnp.sin(ang)
    return jnp.concatenate([cos, cos], axis=-1), jnp.concatenate([-sin, sin], axis=-1)


def _mlp_kernel(z_ref, h_ref, wu_ref, wd_ref, o_ref):
    f = pl.program_id(1)

    @pl.when(f == 0)
    def _():
        o_ref[...] = h_ref[...]

    hm = jnp.dot(z_ref[...], wu_ref[...], preferred_element_type=F32)
    hm = jnp.square(jnp.maximum(hm, 0.0)).astype(BF16)
    o_ref[...] += jnp.dot(hm, wd_ref[...], preferred_element_type=F32)


def _mlp(z, h, w_up, w_down):
    m, d = z.shape
    dff = w_up.shape[1]
    tm = _tile(m, 512)
    tf = _tile(dff, 512)
    once = pl.Buffered(1)
    return pl.pallas_call(
        _mlp_kernel,
        grid=(m // tm, dff // tf),
        in_specs=[
            pl.BlockSpec((tm, d), lambda i, f: (i, 0), pipeline_mode=once),
            pl.BlockSpec((tm, d), lambda i, f: (i, 0), pipeline_mode=once),
            pl.BlockSpec((d, tf), lambda i, f: (0, f)),
            pl.BlockSpec((tf, d), lambda i, f: (f, 0)),
        ],
        out_specs=pl.BlockSpec((tm, d), lambda i, f: (i, 0)),
        out_shape=jax.ShapeDtypeStruct((m, d), F32),
        compiler_params=_params("parallel", "arbitrary"),
        name="mlp",
    )(z, h, w_up, w_down)


def _ln_swish(y, g, beta):
    mu = jnp.mean(y, axis=-1, keepdims=True)
    xc = y - mu
    var = jnp.mean(xc * xc, axis=-1, keepdims=True)
    yn = xc * lax.rsqrt(var + NORM_EPS) * g + beta
    return yn * jax.nn.sigmoid(yn)


def _conv_prompt_kernel(cur_ref, halo_ref, w_ref, b_ref, g_ref, beta_ref, o_ref, ext_ref, y_ref, *, kw, rc, cc):
    t = pl.program_id(1)
    tt, c = cur_ref.shape
    ext_ref[0:CONV_HALO, :] = jnp.where(t == 0, 0.0, halo_ref[...])
    ext_ref[CONV_HALO:CONV_HALO + tt, :] = cur_ref[...]
    off = CONV_HALO - (kw - 1)
    for r0 in range(0, tt, rc):
        for c0 in range(0, c, cc):
            acc = ext_ref[r0 + off:r0 + off + rc, c0:c0 + cc] * w_ref[0:1, c0:c0 + cc]
            for k in range(1, kw):
                acc = acc + ext_ref[r0 + off + k:r0 + off + k + rc, c0:c0 + cc] * w_ref[k:k + 1, c0:c0 + cc]
            y_ref[r0:r0 + rc, c0:c0 + cc] = acc + b_ref[:, c0:c0 + cc]
    o_ref[...] = _ln_swish(y_ref[...], g_ref[...], beta_ref[...]).astype(o_ref.dtype)


def _conv_prompt(u, batch, seq, w_dw, b_dw, ln_g, ln_b):
    m, c = u.shape
    kw = w_dw.shape[0]
    assert kw - 1 <= CONV_HALO
    tt = _tile(seq, 128)
    assert tt % CONV_HALO == 0
    nt = seq // tt
    hb = tt // CONV_HALO
    vec = lambda a: a.reshape(1, c)
    vspec = pl.BlockSpec((1, c), lambda b, t: (0, 0))
    kern = functools.partial(_conv_prompt_kernel, kw=kw, rc=min(32, tt), cc=min(512, c))
    return pl.pallas_call(
        kern,
        grid=(batch, nt),
        in_specs=[
            pl.BlockSpec((tt, c), lambda b, t: (b * nt + t, 0)),
            pl.BlockSpec((CONV_HALO, c), lambda b, t: (jnp.maximum((b * nt + t) * hb - 1, 0), 0)),
            pl.BlockSpec((kw, c), lambda b, t: (0, 0)),
            vspec, vspec, vspec,
        ],
        out_specs=pl.BlockSpec((tt, c), lambda b, t: (b * nt + t, 0)),
        out_shape=jax.ShapeDtypeStruct((m, c), BF16),
        scratch_shapes=[pltpu.VMEM((CONV_HALO + tt, c), F32), pltpu.VMEM((tt, c), F32)],
        compiler_params=_params("parallel", "parallel"),
        name="conv_prompt",
    )(u, u, w_dw, vec(b_dw), vec(ln_g), vec(ln_b))


def _conv_sample_kernel(st_ref, u_ref, w_ref, b_ref, g_ref, beta_ref, o_ref, ns_ref):
    ks = st_ref.shape[0]
    st = st_ref[...]
    u = u_ref[...]
    y = jnp.sum(st * w_ref[0:ks, :], axis=0, keepdims=True) + u * w_ref[ks:ks + 1, :] + b_ref[...]
    o_ref[...] = _ln_swish(y, g_ref[...], beta_ref[...]).astype(o_ref.dtype)
    ns_ref[0:ks - 1, :] = st_ref[1:ks, :]
    ns_ref[ks - 1:ks, :] = u


def _conv_sample(u, state, w_dw, b_dw, ln_g, ln_b):
    db, c = u.shape
    ks = state.shape[1]
    vec = lambda a: a.reshape(1, c)
    vspec = pl.BlockSpec((1, c), lambda b: (0, 0))
    out, new_state = pl.pallas_call(
        _conv_sample_kernel,
        grid=(db,),
        in_specs=[
            pl.BlockSpec((None, ks, c), lambda b: (b, 0, 0)),
            pl.BlockSpec((None, 1, c), lambda b: (b, 0, 0)),
            pl.BlockSpec((ks + 1, c), lambda b: (0, 0)),
            vspec, vspec, vspec,
        ],
        out_specs=[pl.BlockSpec((None, 1, c), lambda b: (b, 0, 0)), pl.BlockSpec((None, ks, c), lambda b: (b, 0, 0))],
        out_shape=[jax.ShapeDtypeStruct((db, 1, c), F32), jax.ShapeDtypeStruct((db, ks, c), F32)],
        compiler_params=_params("parallel"),
        name="conv_sample",
    )(state, u.reshape(db, 1, c), w_dw, vec(b_dw), vec(ln_g), vec(ln_b))
    return out.reshape(db, c), new_state


def _attn_block_body(jj, q_ref, kb_ref, vb_ref, km_ref, o_ref, scale):
    blk = MOBA_BLOCK
    q = q_ref[...]
    ln = (jj + 1) * blk
    s = lax.dot_general(q.astype(BF16), kb_ref[0:ln, :], (((1,), (1,)), ((), ())),
                        preferred_element_type=F32) * scale
    pieces = []
    if jj > MOBA_TOPK:
        g = [jnp.sum(q * km_ref[n:n + 1, :], axis=1, keepdims=True) for n in range(jj)]
        for n in range(jj):
            cnt = jnp.zeros_like(g[n])
            for m in range(jj):
                if m == n:
                    continue
                beats = (g[m] >= g[n]) if m < n else (g[m] > g[n])
                cnt = cnt + beats.astype(F32)
            pieces.append(jnp.where(cnt < MOBA_TOPK, s[:, n * blk:(n + 1) * blk], -jnp.inf))
    else:
        pieces = [s[:, n * blk:(n + 1) * blk] for n in range(jj)]
    row = lax.broadcasted_iota(jnp.int32, (blk, blk), 0)
    col = lax.broadcasted_iota(jnp.int32, (blk, blk), 1)
    pieces.append(jnp.where(col <= row, s[:, jj * blk:ln], -jnp.inf))
    mx = jnp.max(pieces[-1], axis=1, keepdims=True)
    for pc in pieces[:-1]:
        mx = jnp.maximum(mx, jnp.max(pc, axis=1, keepdims=True))
    ps = [jnp.exp(pc - mx) for pc in pieces]
    den = ps[0].sum(axis=1, keepdims=True)
    for p in ps[1:]:
        den = den + p.sum(axis=1, keepdims=True)
    pcat = jnp.concatenate(ps, axis=1) if len(ps) > 1 else ps[0]
    acc = jnp.dot(pcat.astype(BF16), vb_ref[0:ln, :], preferred_element_type=F32)
    o_ref[...] = (acc / den).astype(o_ref.dtype)


def _attn_prompt_kernel(q_ref, k_ref, v_ref, o_ref, kb_ref, vb_ref, km_ref, *, nb, scale):
    j = pl.program_id(2)

    @pl.when(j == 0)
    def _():
        kb_ref[...] = k_ref[...].astype(BF16)
        vb_ref[...] = v_ref[...].astype(BF16)
        for n in range(nb):
            km_ref[n:n + 1, :] = jnp.mean(k_ref[n * MOBA_BLOCK:(n + 1) * MOBA_BLOCK, :], axis=0, keepdims=True)

    for jj in range(nb):
        pl.when(j == jj)(functools.partial(_attn_block_body, jj, q_ref, kb_ref, vb_ref, km_ref, o_ref, scale))


def _attn_prompt(q, k, v, batch, seq, n_heads, hd):
    assert seq % MOBA_BLOCK == 0
    nb = seq // MOBA_BLOCK
    m = batch * seq
    kern = functools.partial(_attn_prompt_kernel, nb=nb, scale=hd ** -0.5)
    kv_spec = pl.BlockSpec((seq, hd), lambda b, h, j: (b, h))
    q_spec = pl.BlockSpec((MOBA_BLOCK, hd), lambda b, h, j: (b * nb + j, h))
    return pl.pallas_call(
        kern,
        grid=(batch, n_heads, nb),
        in_specs=[q_spec, kv_spec, kv_spec],
        out_specs=q_spec,
        out_shape=jax.ShapeDtypeStruct((m, n_heads * hd), BF16),
        scratch_shapes=[pltpu.VMEM((seq, hd), BF16), pltpu.VMEM((seq, hd), BF16), pltpu.VMEM((nb, hd), F32)],
        compiler_params=_params("parallel", "parallel", "arbitrary"),
        name="attn_prompt",
    )(q, k, v)


def _kmean_kernel(pt_ref, *refs, inv_count):
    o_ref = refs[-1]
    s = jnp.sum(refs[0][...], axis=0)
    for r in refs[1:-1]:
        s = s + jnp.sum(r[...], axis=0)
    o_ref[...] = s * inv_count


def _paged_kmean(cache_k, layer, pt_flat, db, n_pages, ppb):
    _, _, ps, n_heads, hd = cache_k.shape
    nblk = n_pages // ppb
    in_specs = [
        pl.BlockSpec((None, None, ps, n_heads, hd),
                     functools.partial(lambda b, n, pt, e: (layer, pt[b * n_pages + n * ppb + e], 0, 0, 0), e=e))
        for e in range(ppb)
    ]
    return pl.pallas_call(
        functools.partial(_kmean_kernel, inv_count=1.0 / (ppb * ps)),
        grid_spec=pltpu.PrefetchScalarGridSpec(
            num_scalar_prefetch=1,
            grid=(db, nblk),
            in_specs=in_specs,
            out_specs=pl.BlockSpec((None, None, n_heads, hd), lambda b, n, pt: (b, n, 0, 0)),
        ),
        out_shape=jax.ShapeDtypeStruct((db, nblk, n_heads, hd), F32),
        compiler_params=_params("parallel", "parallel"),
        name="paged_kmean",
    )(pt_flat, *([cache_k] * ppb))


def _topk_kernel(km_ref, q_ref, o_ref):
    nblk = km_ref.shape[0]
    gate = jnp.sum(km_ref[...] * q_ref[...][None], axis=-1, keepdims=True)
    blk_id = lax.broadcasted_iota(jnp.int32, gate.shape, 0)
    for s in range(MOBA_TOPK):
        mx = jnp.max(gate, axis=0, keepdims=True)
        idx = jnp.min(jnp.where(gate == mx, blk_id, nblk), axis=0, keepdims=True)
        o_ref[s:s + 1] = idx
        gate = jnp.where(blk_id == idx, -jnp.inf, gate)


def _sample_topk(kmean, q):
    db, nblk, n_heads, hd = kmean.shape
    assert nblk >= MOBA_TOPK
    out = pl.pallas_call(
        _topk_kernel,
        grid=(db,),
        in_specs=[pl.BlockSpec((None, nblk, n_heads, hd), lambda b: (b, 0, 0, 0)),
                  pl.BlockSpec((None, n_heads, hd), lambda b: (b, 0, 0))],
        out_specs=pl.BlockSpec((None, MOBA_TOPK, n_heads, 1), lambda b: (b, 0, 0, 0)),
        out_shape=jax.ShapeDtypeStruct((db, MOBA_TOPK, n_heads, 1), jnp.int32),
        compiler_params=_params("parallel"),
        name="sample_topk",
    )(kmean, q)
    return jnp.swapaxes(out.reshape(db, MOBA_TOPK, n_heads), 1, 2)


def _attn_sample_kernel(pt_ref, sel_ref, q_ref, kn_ref, vn_ref, ck_hbm, cv_hbm, o_ref, kbuf, vbuf, sems, *,
                        layer, n_pages, ppb, scale):
    b = pl.program_id(0)
    n_heads, hd = q_ref.shape
    n_slab = MOBA_TOPK * ppb

    def slab_copies(h, slot):
        cps = []
        for s in range(MOBA_TOPK):
            blk = sel_ref[(b * n_heads + h) * MOBA_TOPK + s]
            for e in range(ppb):
                page = pt_ref[b * n_pages + blk * ppb + e]
                i = s * ppb + e
                cps.append(pltpu.make_async_copy(ck_hbm.at[layer, page, :, h, :], kbuf.at[slot, i], sems.at[0, slot, i]))
                cps.append(pltpu.make_async_copy(cv_hbm.at[layer, page, :, h, :], vbuf.at[slot, i], sems.at[1, slot, i]))
        return cps

    for cp in slab_copies(0, 0):
        cp.start()
    for h in range(n_heads):
        slot = h % 2
        if h + 1 < n_heads:
            for cp in slab_copies(h + 1, 1 - slot):
                cp.start()
        for cp in slab_copies(h, slot):
            cp.wait()
        q = q_ref[h:h + 1, :]
        q8 = jnp.broadcast_to(q, (SUBLANES, hd)).astype(BF16)
        ss = [lax.dot_general(q8, kbuf[slot, i].astype(BF16), (((1,), (1,)), ((), ())),
                              preferred_element_type=F32) * scale for i in range(n_slab)]
        s_self = jnp.sum(q * kn_ref[h:h + 1, :], axis=1, keepdims=True) * scale
        mx = s_self
        for s in ss:
            mx = jnp.maximum(mx, jnp.max(s, axis=1, keepdims=True))
        p_self = jnp.exp(s_self - mx)
        den = p_self
        acc = p_self * vn_ref[h:h + 1, :]
        for i, s in enumerate(ss):
            p = jnp.exp(s - mx)
            den = den + jnp.sum(p, axis=1, keepdims=True)
            acc = acc + jnp.dot(p.astype(BF16), vbuf[slot, i].astype(BF16), preferred_element_type=F32)
        o_ref[h:h + 1, :] = (acc / den)[0:1, :]


def _attn_sample(q, k_new, v_new, cache_k, cache_v, layer, pt_flat, sel_flat, n_pages, ppb):
    db, n_heads, hd = q.shape
    ps = cache_k.shape[2]
    n_slab = MOBA_TOPK * ppb
    row_spec = pl.BlockSpec((None, n_heads, hd), lambda b, pt, sel: (b, 0, 0))
    any_spec = pl.BlockSpec(memory_space=pl.ANY)
    out = pl.pallas_call(
        functools.partial(_attn_sample_kernel, layer=layer, n_pages=n_pages, ppb=ppb, scale=hd ** -0.5),
        grid_spec=pltpu.PrefetchScalarGridSpec(
            num_scalar_prefetch=2,
            grid=(db,),
            in_specs=[row_spec, row_spec, row_spec, any_spec, any_spec],
            out_specs=row_spec,
            scratch_shapes=[pltpu.VMEM((2, n_slab, ps, hd), F32), pltpu.VMEM((2, n_slab, ps, hd), F32),
                            pltpu.SemaphoreType.DMA((2, 2, n_slab))],
        ),
        out_shape=jax.ShapeDtypeStruct((db, n_heads, hd), F32),
        compiler_params=_params("arbitrary"),
        name="attn_sample",
    )(pt_flat, sel_flat, q, k_new, v_new, cache_k, cache_v)
    return out.reshape(db, n_heads * hd)


def _in_projection(z, w_in, c, a, cos, sin_signed, table_rows, hd):
    m = z.shape[0]
    tm = _tile(m, 512)
    tn_glu = _tile(c, 512)
    tn = _tile(a, 1024)
    assert table_rows % tm == 0
    period = table_rows // tm
    tbl = lambda t: (t, (tm, hd), lambda j, i: (i % period, 0))
    u = _matmul([z], [(0, w_in, 0, 0, 0), (0, w_in, 0, c, 1)], [], _ep_glu, c, tn_glu, F32, "in_proj_glu")
    q = _matmul([z], [(0, w_in, 0, 2 * c, 0)], [tbl(cos), tbl(sin_signed)], _ep_rope, a, tn, F32, "in_proj_q")
    k = _matmul([z], [(0, w_in, 0, 2 * c + a, 0)], [tbl(cos), tbl(sin_signed)], _ep_rope, a, tn, F32, "in_proj_k")
    v = _matmul([z], [(0, w_in, 0, 2 * c + 2 * a, 0)], [], _ep_plain, a, tn, F32, "in_proj_v")
    return u, q, k, v


def _post_mixer(x, conv_out, attn, p, w_out, g_ffn, w_up, w_down, g_ple, w_ple_gate, w_ple_proj):
    m, d = x.shape
    c = conv_out.shape[1]
    assert attn.shape[1] == c
    tm = _tile(m, 512)
    tn = _tile(d, 1024)
    resid = lambda r: (r, (tm, tn), lambda j, i: (i, j))
    h = _matmul([conv_out, attn], [(0, w_out, 0, 0, 0), (1, w_out, 1, 0, 0)], [resid(x)], _ep_resid, d, tn, F32,
                "out_proj")
    h = _mlp(_rmsnorm(h, g_ffn, BF16), h, w_up, w_down)
    z = _rmsnorm(h, g_ple, BF16)
    return _matmul([z, p], [(0, w_ple_gate, 0, 0, 0), (1, w_ple_proj, 0, 0, 1)], [resid(h)], _ep_ple, d, tn, F32,
                   "ple")


def kernel(x_prompt, x_sample, cache_k, cache_v, state_conv, page_table, p_prompt, p_sample, g_mix, w_in, w_dw, b_dw, g_conv_ln, b_conv_ln, w_out, g_ffn, w_up, w_down, g_ple, w_ple_gate, w_ple_proj, g_final):
    batch, seq, d = x_prompt.shape
    db, dseq, _ = x_sample.shape
    assert dseq == 1
    depth, n_pool, page, n_heads, hd = cache_k.shape
    n_pages = page_table.shape[1]
    past_len = n_pages * page
    assert MOBA_BLOCK % page == 0 and past_len % MOBA_BLOCK == 0
    ppb = MOBA_BLOCK // page
    c = w_dw.shape[-1]
    a = n_heads * hd
    assert c + a == d

    hp = x_prompt.reshape(batch * seq, d)
    hs = x_sample.reshape(db, d)
    cos_p, sin_p = _rope_tables(0, seq, hd)
    cos_s, sin_s = _rope_tables(past_len, 1, hd)
    cos_s = jnp.broadcast_to(cos_s, (db, hd))
    sin_s = jnp.broadcast_to(sin_s, (db, hd))
    pt_flat = page_table.reshape(-1).astype(jnp.int32)

    kp_l, vp_l, cp_l, ks_l, vs_l, cs_l = [], [], [], [], [], []
    for l in range(depth):
        wb = lambda w: w[l].astype(BF16)
        w_in_b, w_out_b, w_up_b, w_down_b = wb(w_in), wb(w_out), wb(w_up), wb(w_down)
        w_pg_b, w_pp_b = wb(w_ple_gate), wb(w_ple_proj)
        post = (w_out_b, g_ffn[l], w_up_b, w_down_b, g_ple[l], w_pg_b, w_pp_b)
        conv_w = (w_dw[l], b_dw[l], g_conv_ln[l], b_conv_ln[l])

        z = _rmsnorm(hp, g_mix[l], BF16)
        u, q, k, v = _in_projection(z, w_in_b, c, a, cos_p, sin_p, seq, hd)
        conv_out = _conv_prompt(u, batch, seq, *conv_w)
        attn = _attn_prompt(q, k, v, batch, seq, n_heads, hd)
        hp = _post_mixer(hp, conv_out, attn, p_prompt[l].reshape(batch * seq, -1), *post)
        kp_l.append(k.reshape(batch, seq, n_heads, hd))
        vp_l.append(v.reshape(batch, seq, n_heads, hd))
        cp_l.append(u.reshape(batch, seq, c)[:, seq - (w_dw.shape[1] - 1):, :])

        z = _rmsnorm(hs, g_mix[l], BF16)
        u, q, k, v = _in_projection(z, w_in_b, c, a, cos_s, sin_s, db, hd)
        conv_out, new_state = _conv_sample(u, state_conv[l], *conv_w)
        kmean = _paged_kmean(cache_k, l, pt_flat, db, n_pages, ppb)
        h3 = lambda t: t.reshape(db, n_heads, hd)
        sel = _sample_topk(kmean, h3(q))
        attn = _attn_sample(h3(q), h3(k), h3(v), cache_k, cache_v, l, pt_flat, sel.reshape(-1), n_pages, ppb)
        hs = _post_mixer(hs, conv_out.astype(BF16), attn.astype(BF16), p_sample[l].reshape(db, -1), *post)
        ks_l.append(k.reshape(db, 1, n_heads, hd))
        vs_l.append(v.reshape(db, 1, n_heads, hd))
        cs_l.append(new_state)

    y_prompt = _rmsnorm(hp, g_final, F32).reshape(batch, seq, d)
    y_sample = _rmsnorm(hs, g_final, F32).reshape(db, 1, d)
    stack = lambda xs: xs[0][None] if len(xs) == 1 else jnp.stack(xs)
    return (y_prompt, y_sample, stack(kp_l), stack(vp_l), stack(cp_l), stack(ks_l), stack(vs_l), stack(cs_l))
```

```python
import functools

import jax
import jax.numpy as jnp
from jax import lax
from jax.experimental import pallas as pl
from jax.experimental.pallas import tpu as pltpu

F32 = jnp.float32
BF16 = jnp.bfloat16

MOBA_BLOCK = 256
MOBA_TOPK = 3
ROPE_THETA = 10000.0
NORM_EPS = 1e-6
LOG2_E = 1.4426950408889634
LANES = 128
SUBLANES = 8
CONV_HALO = 32
VMEM_LIMIT = 60 * 1024 * 1024


def _params(*sem):
    return pltpu.CompilerParams(dimension_semantics=sem, vmem_limit_bytes=VMEM_LIMIT)


def _tile(n, pref):
    if n <= pref:
        return n
    t = pref
    while n % t:
        t //= 2
    return t


def _rmsnorm_kernel(x_ref, g_ref, o_ref):
    x = x_ref[...]
    ms = jnp.mean(x * x, axis=-1, keepdims=True)
    o_ref[...] = (x * lax.rsqrt(ms + NORM_EPS) * g_ref[...]).astype(o_ref.dtype)


def _rmsnorm(x, g, out_dtype):
    m, d = x.shape
    tm = _tile(m, 256)
    return pl.pallas_call(
        _rmsnorm_kernel,
        grid=(m // tm,),
        in_specs=[pl.BlockSpec((tm, d), lambda i: (i, 0)), pl.BlockSpec((1, d), lambda i: (0, 0))],
        out_specs=pl.BlockSpec((tm, d), lambda i: (i, 0)),
        out_shape=jax.ShapeDtypeStruct((m, d), out_dtype),
        compiler_params=_params("parallel"),
        name="rmsnorm",
    )(x, g.reshape(1, d))


def _mm_kernel(*refs, n_lhs, dots, n_extra, epilogue):
    lhs = refs[:n_lhs]
    ws = refs[n_lhs:n_lhs + len(dots)]
    extras = refs[n_lhs + len(dots):n_lhs + len(dots) + n_extra]
    out = refs[-1]
    accs = {}
    for (li, group), w in zip(dots, ws):
        a = lhs[li][...]
        if a.dtype != BF16:
            a = a.astype(BF16)
        d = jnp.dot(a, w[...], preferred_element_type=F32)
        accs[group] = d if group not in accs else accs[group] + d
    res = epilogue([accs[g] for g in sorted(accs)], [e[...] for e in extras])
    out[...] = res.astype(out.dtype)


def _matmul(lhs, dots, extras, epilogue, n_cols, tn, out_dtype, name, tm_pref=512):
    m = lhs[0].shape[0]
    tm = _tile(m, tm_pref)
    in_specs = [pl.BlockSpec((tm, a.shape[1]), lambda j, i: (i, 0)) for a in lhs]
    operands = list(lhs)
    for li, w, rb, co, _ in dots:
        kl = lhs[li].shape[1]
        assert co % tn == 0
        in_specs.append(pl.BlockSpec((kl, tn), functools.partial(lambda j, i, rb, cb: (rb, cb + j), rb=rb, cb=co // tn)))
        operands.append(w)
    for arr, blk, imap in extras:
        in_specs.append(pl.BlockSpec(blk, imap))
        operands.append(arr)
    kern = functools.partial(_mm_kernel, n_lhs=len(lhs), dots=[(d[0], d[4]) for d in dots],
                             n_extra=len(extras), epilogue=epilogue)
    return pl.pallas_call(
        kern,
        grid=(n_cols // tn, m // tm),
        in_specs=in_specs,
        out_specs=pl.BlockSpec((tm, tn), lambda j, i: (i, j)),
        out_shape=jax.ShapeDtypeStruct((m, n_cols), out_dtype),
        compiler_params=_params("parallel", "parallel"),
        name=name,
    )(*operands)


def _ep_plain(accs, extras):
    return accs[0]


def _ep_glu(accs, extras):
    return accs[0] * jax.nn.sigmoid(accs[1])


def _ep_resid(accs, extras):
    return extras[0] + accs[0]


def _ep_ple(accs, extras):
    return extras[0] + jax.nn.sigmoid(accs[0]) * accs[1]


def _ep_rope(accs, extras):
    cos, sin_signed = extras
    y = accs[0]
    hd = cos.shape[1]
    outs = []
    for h in range(y.shape[1] // hd):
        xh = y[:, h * hd:(h + 1) * hd]
        outs.append(xh * cos + pltpu.roll(xh, hd // 2, 1) * sin_signed)
    return jnp.concatenate(outs, axis=1)


def _rope_tables(pos0, s, hd):
    inv = 1.0 / (ROPE_THETA ** (jnp.arange(0, hd, 2, dtype=F32) / hd))
    pos = (jnp.arange(s, dtype=jnp.int32) + pos0).astype(F32)
    ang = pos[:, None] * inv[None, :]
    cos, sin = jnp.cos(ang), jnp.sin(ang)
    return jnp.concatenate([cos, cos], axis=-1), jnp.concatenate([-sin, sin], axis=-1)


def _mlp_kernel(pt_ref, z_ref, h_ref, wu_ref, wd_ref, *refs, n_page, ppb):
    page_refs = refs[:n_page]
    o_ref = refs[n_page]
    f = pl.program_id(1)

    @pl.when(f == 0)
    def _():
        o_ref[...] = h_ref[...]

    hm = jnp.dot(z_ref[...], wu_ref[...], preferred_element_type=F32)
    hm = jnp.square(jnp.maximum(hm, 0.0)).astype(BF16)
    o_ref[...] += jnp.dot(hm, wd_ref[...], preferred_element_type=F32)

    if n_page:
        km_ref = refs[n_page + 1]
        ps = page_refs[0].shape[0]
        for r in range(n_page // ppb):
            s = jnp.sum(page_refs[r * ppb][...], axis=0)
            for e in range(1, ppb):
                s = s + jnp.sum(page_refs[r * ppb + e][...], axis=0)
            km_ref[r] = s * (1.0 / (ppb * ps))


def _mlp(z, h, w_up, w_down, paged=None):
    m, d = z.shape
    dff = w_up.shape[1]
    tm = _tile(m, 512)
    tf = _tile(dff, 512)
    nf = dff // tf
    steps = (m // tm) * nf
    once = pl.Buffered(1)
    in_specs = [
        pl.BlockSpec((tm, d), lambda i, f, pt: (i, 0), pipeline_mode=once),
        pl.BlockSpec((tm, d), lambda i, f, pt: (i, 0), pipeline_mode=once),
        pl.BlockSpec((d, tf), lambda i, f, pt: (0, f)),
        pl.BlockSpec((tf, d), lambda i, f, pt: (f, 0)),
    ]
    out_specs = [pl.BlockSpec((tm, d), lambda i, f, pt: (i, 0))]
    out_shape = [jax.ShapeDtypeStruct((m, d), F32)]
    operands = [z, h, w_up, w_down]
    n_page, ppb = 0, 1
    if paged is None:
        pt_flat = jnp.zeros((1,), jnp.int32)
    else:
        cache_k, layer, pt_flat, ppb = paged
        _, _, ps, n_heads, hd = cache_k.shape
        n_slots = pt_flat.shape[0]
        assert n_slots % (steps * ppb) == 0
        n_page = n_slots // steps
        for e in range(n_page):
            in_specs.append(pl.BlockSpec(
                (None, None, ps, n_heads, hd),
                functools.partial(lambda i, f, pt, e: (layer, pt[(i * nf + f) * n_page + e], 0, 0, 0), e=e)))
        operands += [cache_k] * n_page
        out_specs.append(pl.BlockSpec((n_page // ppb, n_heads, hd), lambda i, f, pt: (i * nf + f, 0, 0)))
        out_shape.append(jax.ShapeDtypeStruct((n_slots // ppb, n_heads, hd), F32))
    outs = pl.pallas_call(
        functools.partial(_mlp_kernel, n_page=n_page, ppb=ppb),
        grid_spec=pltpu.PrefetchScalarGridSpec(
            num_scalar_prefetch=1, grid=(m // tm, nf), in_specs=in_specs, out_specs=out_specs),
        out_shape=out_shape,
        compiler_params=_params("parallel", "arbitrary"),
        name="mlp",
    )(pt_flat, *operands)
    return outs[0] if paged is None else outs


def _ln_swish(y, g, beta):
    mu = jnp.mean(y, axis=-1, keepdims=True)
    xc = y - mu
    var = jnp.mean(xc * xc, axis=-1, keepdims=True)
    yn = xc * lax.rsqrt(var + NORM_EPS) * g + beta
    return yn * jax.nn.sigmoid(yn)


def _conv_prompt_kernel(cur_ref, halo_ref, w_ref, b_ref, g_ref, beta_ref, o_ref, ext_ref, sh_ref, y_ref, *, kw, cc):
    t = pl.program_id(1)
    tt, c = cur_ref.shape
    ext_ref[0:CONV_HALO, :] = jnp.where(t == 0, 0.0, halo_ref[...])
    ext_ref[CONV_HALO:CONV_HALO + tt, :] = cur_ref[...]
    off = CONV_HALO - (kw - 1)
    rows = sh_ref.shape[1]
    for phase in range(1, SUBLANES):
        sh_ref[phase - 1] = ext_ref[phase:phase + rows, :]
    groups = tt // SUBLANES
    for c0 in range(0, c, cc):
        acc = None
        for k in range(kw):
            phase = (off + k) % SUBLANES
            a0 = off + k - phase
            src = ext_ref if phase == 0 else sh_ref.at[phase - 1]
            term = src[a0:a0 + tt, c0:c0 + cc].reshape(groups, SUBLANES, cc) * w_ref[k, :, c0:c0 + cc][None]
            acc = term if acc is None else acc + term
        y_ref[:, c0:c0 + cc] = acc.reshape(tt, cc) + b_ref[:, c0:c0 + cc]
    o_ref[...] = _ln_swish(y_ref[...], g_ref[...], beta_ref[...]).astype(o_ref.dtype)


def _conv_prompt(u, batch, seq, w_dw, b_dw, ln_g, ln_b):
    m, c = u.shape
    kw = w_dw.shape[0]
    assert kw - 1 <= CONV_HALO
    tt = _tile(seq, 128)
    assert tt % CONV_HALO == 0
    nt = seq // tt
    hb = tt // CONV_HALO
    vec = lambda a: a.reshape(1, c)
    vspec = pl.BlockSpec((1, c), lambda b, t: (0, 0))
    kern = functools.partial(_conv_prompt_kernel, kw=kw, cc=min(LANES, c))
    w_rep = jnp.broadcast_to(w_dw[:, None, :], (kw, SUBLANES, c))
    return pl.pallas_call(
        kern,
        grid=(batch, nt),
        in_specs=[
            pl.BlockSpec((tt, c), lambda b, t: (b * nt + t, 0)),
            pl.BlockSpec((CONV_HALO, c), lambda b, t: (jnp.maximum((b * nt + t) * hb - 1, 0), 0)),
            pl.BlockSpec((kw, SUBLANES, c), lambda b, t: (0, 0, 0)),
            vspec, vspec, vspec,
        ],
        out_specs=pl.BlockSpec((tt, c), lambda b, t: (b * nt + t, 0)),
        out_shape=jax.ShapeDtypeStruct((m, c), BF16),
        scratch_shapes=[pltpu.VMEM((CONV_HALO + tt, c), F32),
                        pltpu.VMEM((SUBLANES - 1, CONV_HALO + tt - SUBLANES, c), F32),
                        pltpu.VMEM((tt, c), F32)],
        compiler_params=_params("parallel", "parallel"),
        name="conv_prompt",
    )(u, u, w_rep, vec(b_dw), vec(ln_g), vec(ln_b))


def _conv_sample_kernel(st_ref, u_ref, w_ref, b_ref, g_ref, beta_ref, o_ref, ns_ref):
    ks = st_ref.shape[0]
    st = st_ref[...]
    u = u_ref[...]
    y = jnp.sum(st * w_ref[0:ks, :], axis=0, keepdims=True) + u * w_ref[ks:ks + 1, :] + b_ref[...]
    o_ref[...] = _ln_swish(y, g_ref[...], beta_ref[...]).astype(o_ref.dtype)
    ns_ref[0:ks - 1, :] = st_ref[1:ks, :]
    ns_ref[ks - 1:ks, :] = u


def _conv_sample(u, state, w_dw, b_dw, ln_g, ln_b):
    db, c = u.shape
    ks = state.shape[1]
    vec = lambda a: a.reshape(1, c)
    vspec = pl.BlockSpec((1, c), lambda b: (0, 0))
    out, new_state = pl.pallas_call(
        _conv_sample_kernel,
        grid=(db,),
        in_specs=[
            pl.BlockSpec((None, ks, c), lambda b: (b, 0, 0)),
            pl.BlockSpec((None, 1, c), lambda b: (b, 0, 0)),
            pl.BlockSpec((ks + 1, c), lambda b: (0, 0)),
            vspec, vspec, vspec,
        ],
        out_specs=[pl.BlockSpec((None, 1, c), lambda b: (b, 0, 0)), pl.BlockSpec((None, ks, c), lambda b: (b, 0, 0))],
        out_shape=[jax.ShapeDtypeStruct((db, 1, c), F32), jax.ShapeDtypeStruct((db, ks, c), F32)],
        compiler_params=_params("parallel"),
        name="conv_sample",
    )(state, u.reshape(db, 1, c), w_dw, vec(b_dw), vec(ln_g), vec(ln_b))
    return out.reshape(db, c), new_state


def _attn_block_body(jj, q_ref, kb_ref, vt_ref, km_ref, o_ref, scale):
    blk = MOBA_BLOCK
    q = q_ref[jj * blk:(jj + 1) * blk, :]
    qb = q.astype(BF16)
    qs = (q * (scale * LOG2_E)).astype(BF16)
    ln = (jj + 1) * blk
    nt = (((1,), (1,)), ((), ()))
    st = lax.dot_general(kb_ref[0:ln, :], qs, nt, preferred_element_type=F32)
    pieces = []
    if jj > MOBA_TOPK:
        g = lax.dot_general(km_ref[...].astype(BF16), qb, nt, preferred_element_type=F32)
        for n in range(jj):
            cnt = jnp.zeros((1, blk), F32)
            for m in range(jj):
                if m == n:
                    continue
                beats = (g[m:m + 1] >= g[n:n + 1]) if m < n else (g[m:m + 1] > g[n:n + 1])
                cnt = cnt + jnp.where(beats, 1.0, 0.0)
            pieces.append(jnp.where(cnt < MOBA_TOPK, st[n * blk:(n + 1) * blk, :], -jnp.inf))
    else:
        pieces = [st[n * blk:(n + 1) * blk, :] for n in range(jj)]
    key = lax.broadcasted_iota(jnp.int32, (blk, blk), 0)
    qry = lax.broadcasted_iota(jnp.int32, (blk, blk), 1)
    pieces.append(jnp.where(key <= qry, st[jj * blk:ln, :], -jnp.inf))
    mx = jnp.max(pieces[-1], axis=0, keepdims=True)
    for pc in pieces[:-1]:
        mx = jnp.maximum(mx, jnp.max(pc, axis=0, keepdims=True))
    ps = [jnp.exp2(pc - mx) for pc in pieces]
    den = ps[0].sum(axis=0, keepdims=True)
    for p in ps[1:]:
        den = den + p.sum(axis=0, keepdims=True)
    pcat = jnp.concatenate(ps, axis=0) if len(ps) > 1 else ps[0]
    acc_t = jnp.dot(vt_ref[:, 0:ln], pcat.astype(BF16), preferred_element_type=F32)
    o_ref[jj * blk:(jj + 1) * blk, :] = (acc_t / den).T.astype(o_ref.dtype)


def _attn_prompt_kernel(q_ref, k_ref, v_ref, o_ref, kb_ref, vt_ref, km_ref, *, nb, scale):
    blk = MOBA_BLOCK
    kb_ref[...] = k_ref[...].astype(BF16)
    km_ref[...] = jnp.zeros(km_ref.shape, F32)
    for n in range(nb):
        vt_ref[:, n * blk:(n + 1) * blk] = v_ref[n * blk:(n + 1) * blk, :].T.astype(BF16)
        km_ref[n:n + 1, :] = jnp.mean(k_ref[n * blk:(n + 1) * blk, :], axis=0, keepdims=True)
    for jj in range(nb):
        _attn_block_body(jj, q_ref, kb_ref, vt_ref, km_ref, o_ref, scale)


def _attn_prompt(q, k, v, batch, seq, n_heads, hd):
    assert seq % MOBA_BLOCK == 0
    nb = seq // MOBA_BLOCK
    m = batch * seq
    bf16_rows = 2 * SUBLANES
    km_rows = -(-nb // bf16_rows) * bf16_rows
    kern = functools.partial(_attn_prompt_kernel, nb=nb, scale=hd ** -0.5)
    spec = pl.BlockSpec((seq, hd), lambda b, h: (b, h))
    return pl.pallas_call(
        kern,
        grid=(batch, n_heads),
        in_specs=[spec, spec, spec],
        out_specs=spec,
        out_shape=jax.ShapeDtypeStruct((m, n_heads * hd), BF16),
        scratch_shapes=[pltpu.VMEM((seq, hd), BF16), pltpu.VMEM((hd, seq), BF16), pltpu.VMEM((km_rows, hd), F32)],
        compiler_params=_params("parallel", "parallel"),
        name="attn_prompt",
    )(q, k, v)


def _topk_kernel(km_ref, q_ref, o_ref):
    nblk = km_ref.shape[0]
    gate = jnp.sum(km_ref[...] * q_ref[...][None], axis=-1, keepdims=True)
    blk_id = lax.broadcasted_iota(jnp.int32, gate.shape, 0)
    for s in range(MOBA_TOPK):
        mx = jnp.max(gate, axis=0, keepdims=True)
        idx = jnp.min(jnp.where(gate == mx, blk_id, nblk), axis=0, keepdims=True)
        o_ref[s:s + 1] = idx
        gate = jnp.where(blk_id == idx, -jnp.inf, gate)


def _sample_topk(kmean, q):
    db, nblk, n_heads, hd = kmean.shape
    assert nblk >= MOBA_TOPK
    out = pl.pallas_call(
        _topk_kernel,
        grid=(db,),
        in_specs=[pl.BlockSpec((None, nblk, n_heads, hd), lambda b: (b, 0, 0, 0)),
                  pl.BlockSpec((None, n_heads, hd), lambda b: (b, 0, 0))],
        out_specs=pl.BlockSpec((None, MOBA_TOPK, n_heads, 1), lambda b: (b, 0, 0, 0)),
        out_shape=jax.ShapeDtypeStruct((db, MOBA_TOPK, n_heads, 1), jnp.int32),
        compiler_params=_params("parallel"),
        name="sample_topk",
    )(kmean, q)
    return jnp.swapaxes(out.reshape(db, MOBA_TOPK, n_heads), 1, 2)


def _attn_sample_kernel(pt_ref, sel_ref, q_ref, kn_ref, vn_ref, ck_hbm, cv_hbm, o_ref, kbuf, vbuf, sems, *,
                        layer, n_pages, ppb, scale):
    b = pl.program_id(0)
    n_heads, hd = q_ref.shape
    n_slab = MOBA_TOPK * ppb

    def slab_copies(h, slot):
        cps = []
        for s in range(MOBA_TOPK):
            blk = sel_ref[(b * n_heads + h) * MOBA_TOPK + s]
            for e in range(ppb):
                page = pt_ref[b * n_pages + blk * ppb + e]
                i = s * ppb + e
                cps.append(pltpu.make_async_copy(ck_hbm.at[layer, page, :, h, :], kbuf.at[slot, i], sems.at[0, slot, i]))
                cps.append(pltpu.make_async_copy(cv_hbm.at[layer, page, :, h, :], vbuf.at[slot, i], sems.at[1, slot, i]))
        return cps

    for cp in slab_copies(0, 0):
        cp.start()
    for h in range(n_heads):
        slot = h % 2
        if h + 1 < n_heads:
            for cp in slab_copies(h + 1, 1 - slot):
                cp.start()
        for cp in slab_copies(h, slot):
            cp.wait()
        q = q_ref[h:h + 1, :]
        q8 = jnp.broadcast_to(q, (SUBLANES, hd)).astype(BF16)
        ss = [lax.dot_general(q8, kbuf[slot, i].astype(BF16), (((1,), (1,)), ((), ())),
                              preferred_element_type=F32) * scale for i in range(n_slab)]
        s_self = jnp.sum(q * kn_ref[h:h + 1, :], axis=1, keepdims=True) * scale
        mx = s_self
        for s in ss:
            mx = jnp.maximum(mx, jnp.max(s, axis=1, keepdims=True))
        p_self = jnp.exp(s_self - mx)
        den = p_self
        acc = p_self * vn_ref[h:h + 1, :]
        for i, s in enumerate(ss):
            p = jnp.exp(s - mx)
            den = den + jnp.sum(p, axis=1, keepdims=True)
            acc = acc + jnp.dot(p.astype(BF16), vbuf[slot, i].astype(BF16), preferred_element_type=F32)
        o_ref[h:h + 1, :] = (acc / den)[0:1, :]


def _attn_sample(q, k_new, v_new, cache_k, cache_v, layer, pt_flat, sel_flat, n_pages, ppb):
    db, n_heads, hd = q.shape
    ps = cache_k.shape[2]
    n_slab = MOBA_TOPK * ppb
    row_spec = pl.BlockSpec((None, n_heads, hd), lambda b, pt, sel: (b, 0, 0))
    any_spec = pl.BlockSpec(memory_space=pl.ANY)
    out = pl.pallas_call(
        functools.partial(_attn_sample_kernel, layer=layer, n_pages=n_pages, ppb=ppb, scale=hd ** -0.5),
        grid_spec=pltpu.PrefetchScalarGridSpec(
            num_scalar_prefetch=2,
            grid=(db,),
            in_specs=[row_spec, row_spec, row_spec, any_spec, any_spec],
            out_specs=row_spec,
            scratch_shapes=[pltpu.VMEM((2, n_slab, ps, hd), F32), pltpu.VMEM((2, n_slab, ps, hd), F32),
                            pltpu.SemaphoreType.DMA((2, 2, n_slab))],
        ),
        out_shape=jax.ShapeDtypeStruct((db, n_heads, hd), F32),
        compiler_params=_params("arbitrary"),
        name="attn_sample",
    )(pt_flat, sel_flat, q, k_new, v_new, cache_k, cache_v)
    return out.reshape(db, n_heads * hd)


def _in_projection(z, w_in, c, a, cos, sin_signed, table_rows, hd):
    m = z.shape[0]
    tm = _tile(m, 512)
    tn_glu = _tile(c, 512)
    tn = _tile(a, 1024)
    assert table_rows % tm == 0
    period = table_rows // tm
    tbl = lambda t: (t, (tm, hd), lambda j, i: (i % period, 0))
    u = _matmul([z], [(0, w_in, 0, 0, 0), (0, w_in, 0, c, 1)], [], _ep_glu, c, tn_glu, F32, "in_proj_glu")
    q = _matmul([z], [(0, w_in, 0, 2 * c, 0)], [tbl(cos), tbl(sin_signed)], _ep_rope, a, tn, F32, "in_proj_q")
    k = _matmul([z], [(0, w_in, 0, 2 * c + a, 0)], [tbl(cos), tbl(sin_signed)], _ep_rope, a, tn, F32, "in_proj_k")
    v = _matmul([z], [(0, w_in, 0, 2 * c + 2 * a, 0)], [], _ep_plain, a, tn, F32, "in_proj_v")
    return u, q, k, v


def _post_mixer(x, conv_out, attn, p, w_out, g_ffn, w_up, w_down, g_ple, w_ple_gate, w_ple_proj, paged=None):
    m, d = x.shape
    c = conv_out.shape[1]
    assert attn.shape[1] == c
    tm = _tile(m, 512)
    tn = _tile(d, 1024)
    resid = lambda r: (r, (tm, tn), lambda j, i: (i, j))
    h = _matmul([conv_out, attn], [(0, w_out, 0, 0, 0), (1, w_out, 1, 0, 0)], [resid(x)], _ep_resid, d, tn, F32,
                "out_proj")
    h = _mlp(_rmsnorm(h, g_ffn, BF16), h, w_up, w_down, paged)
    if paged is not None:
        h, kmean = h
    z = _rmsnorm(h, g_ple, BF16)
    out = _matmul([z, p], [(0, w_ple_gate, 0, 0, 0), (1, w_ple_proj, 0, 0, 1)], [resid(h)], _ep_ple, d, tn, F32,
                  "ple")
    return out if paged is None else (out, kmean)


def kernel(x_prompt, x_sample, cache_k, cache_v, state_conv, page_table, p_prompt, p_sample, g_mix, w_in, w_dw, b_dw, g_conv_ln, b_conv_ln, w_out, g_ffn, w_up, w_down, g_ple, w_ple_gate, w_ple_proj, g_final):
    batch, seq, d = x_prompt.shape
    db, dseq, _ = x_sample.shape
    assert dseq == 1
    depth, n_pool, page, n_heads, hd = cache_k.shape
    n_pages = page_table.shape[1]
    past_len = n_pages * page
    assert MOBA_BLOCK % page == 0 and past_len % MOBA_BLOCK == 0
    ppb = MOBA_BLOCK // page
    c = w_dw.shape[-1]
    a = n_heads * hd
    assert c + a == d

    hp = x_prompt.reshape(batch * seq, d)
    hs = x_sample.reshape(db, d)
    cos_p, sin_p = _rope_tables(0, seq, hd)
    cos_s, sin_s = _rope_tables(past_len, 1, hd)
    cos_s = jnp.broadcast_to(cos_s, (db, hd))
    sin_s = jnp.broadcast_to(sin_s, (db, hd))
    pt_flat = page_table.reshape(-1).astype(jnp.int32)

    kp_l, vp_l, cp_l, ks_l, vs_l, cs_l = [], [], [], [], [], []
    for l in range(depth):
        wb = lambda w: w[l].astype(BF16)
        w_in_b, w_out_b, w_up_b, w_down_b = wb(w_in), wb(w_out), wb(w_up), wb(w_down)
        w_pg_b, w_pp_b = wb(w_ple_gate), wb(w_ple_proj)
        post = (w_out_b, g_ffn[l], w_up_b, w_down_b, g_ple[l], w_pg_b, w_pp_b)
        conv_w = (w_dw[l], b_dw[l], g_conv_ln[l], b_conv_ln[l])

        z = _rmsnorm(hp, g_mix[l], BF16)
        u, q, k, v = _in_projection(z, w_in_b, c, a, cos_p, sin_p, seq, hd)
        conv_out = _conv_prompt(u, batch, seq, *conv_w)
        attn = _attn_prompt(q, k, v, batch, seq, n_heads, hd)
        hp, kmean = _post_mixer(hp, conv_out, attn, p_prompt[l].reshape(batch * seq, -1), *post,
                                paged=(cache_k, l, pt_flat, ppb))
        kp_l.append(k.reshape(batch, seq, n_heads, hd))
        vp_l.append(v.reshape(batch, seq, n_heads, hd))
        cp_l.append(u.reshape(batch, seq, c)[:, seq - (w_dw.shape[1] - 1):, :])

        z = _rmsnorm(hs, g_mix[l], BF16)
        u, q, k, v = _in_projection(z, w_in_b, c, a, cos_s, sin_s, db, hd)
        conv_out, new_state = _conv_sample(u, state_conv[l], *conv_w)
        h3 = lambda t: t.reshape(db, n_heads, hd)
        sel = _sample_topk(kmean.reshape(db, n_pages // ppb, n_heads, hd), h3(q))
        attn = _attn_sample(h3(q), h3(k), h3(v), cache_k, cache_v, l, pt_flat, sel.reshape(-1), n_pages, ppb)
        hs = _post_mixer(hs, conv_out.astype(BF16), attn.astype(BF16), p_sample[l].reshape(db, -1), *post)
        ks_l.append(k.reshape(db, 1, n_heads, hd))
        vs_l.append(v.reshape(db, 1, n_heads, hd))
        cs_l.append(new_state)

    y_prompt = _rmsnorm(hp, g_final, F32).reshape(batch, seq, d)
    y_sample = _rmsnorm(hs, g_final, F32).reshape(db, 1, d)
    stack = lambda xs: xs[0][None] if len(xs) == 1 else jnp.stack(xs)
    return (y_prompt, y_sample, stack(kp_l), stack(vp_l), stack(cp_l), stack(ks_l), stack(vs_l), stack(cs_l))
```

```python
import functools

import jax
import jax.numpy as jnp
from jax import lax
from jax.experimental import pallas as pl
from jax.experimental.pallas import tpu as pltpu

F32 = jnp.float32
BF16 = jnp.bfloat16

MOBA_BLOCK = 256
MOBA_TOPK = 3
ROPE_THETA = 10000.0
NORM_EPS = 1e-6
LOG2_E = 1.4426950408889634
LANES = 128
SUBLANES = 8
CONV_HALO = 32
MM_ROW_CHUNK = 512
VMEM_LIMIT = 60 * 1024 * 1024


def _params(*sem):
    return pltpu.CompilerParams(dimension_semantics=sem, vmem_limit_bytes=VMEM_LIMIT)


def _tile(n, pref):
    if n <= pref:
        return n
    t = pref
    while n % t:
        t //= 2
    return t


def _rmsnorm_kernel(x_ref, g_ref, o_ref):
    x = x_ref[...]
    ms = jnp.mean(x * x, axis=-1, keepdims=True)
    o_ref[...] = (x * lax.rsqrt(ms + NORM_EPS) * g_ref[...]).astype(o_ref.dtype)


def _rmsnorm(x, g, out_dtype):
    m, d = x.shape
    tm = _tile(m, 256)
    return pl.pallas_call(
        _rmsnorm_kernel,
        grid=(m // tm,),
        in_specs=[pl.BlockSpec((tm, d), lambda i: (i, 0)), pl.BlockSpec((1, d), lambda i: (0, 0))],
        out_specs=pl.BlockSpec((tm, d), lambda i: (i, 0)),
        out_shape=jax.ShapeDtypeStruct((m, d), out_dtype),
        compiler_params=_params("parallel"),
        name="rmsnorm",
    )(x, g.reshape(1, d))


def _mm_kernel(*refs, n_lhs, dots, n_extra, epilogue, side):
    n_dot = len(dots)
    groups = 2 if side else 1
    pos = 0
    lhs = [refs[pos + g * n_lhs:pos + (g + 1) * n_lhs] for g in range(groups)]
    pos += groups * n_lhs
    ws = refs[pos:pos + n_dot]
    pos += n_dot
    extras = [refs[pos + g * n_extra:pos + (g + 1) * n_extra] for g in range(groups)]
    pos += groups * n_extra
    outs = refs[pos:pos + groups]
    wbs = refs[pos + groups:pos + groups + n_dot]
    i = pl.program_id(1)

    @pl.when(i == 0)
    def _():
        for w, wb in zip(ws, wbs):
            wb[...] = w[...].astype(BF16)

    def run(g):
        rows = outs[g].shape[0]
        rc = MM_ROW_CHUNK if rows % MM_ROW_CHUNK == 0 else rows
        for r0 in range(0, rows, rc):
            accs = {}
            for (li, group), wb in zip(dots, wbs):
                a = lhs[g][li][r0:r0 + rc, :]
                if a.dtype != BF16:
                    a = a.astype(BF16)
                d = jnp.dot(a, wb[...], preferred_element_type=F32)
                accs[group] = d if group not in accs else accs[group] + d
            res = epilogue([accs[k] for k in sorted(accs)], [e[r0:r0 + rc, :] for e in extras[g]])
            outs[g][r0:r0 + rc, :] = res.astype(outs[g].dtype)

    run(0)
    if side:
        pl.when(i == pl.num_programs(1) - 1)(functools.partial(run, 1))


def _matmul(lhs, dots, extras, epilogue, n_cols, tn, out_dtype, name, side=None, tm_pref=1024):
    m = lhs[0].shape[0]
    tm = _tile(m, tm_pref)
    in_specs = [pl.BlockSpec((tm, a.shape[1]), lambda j, i: (i, 0)) for a in lhs]
    operands = list(lhs)
    if side:
        lhs_s, extras_s = side
        ms = lhs_s[0].shape[0]
        in_specs += [pl.BlockSpec((ms, a.shape[1]), lambda j, i: (0, 0)) for a in lhs_s]
        operands += list(lhs_s)
    scratch = []
    for li, w, layer, rb, co, _ in dots:
        kl = lhs[li].shape[1]
        assert co % tn == 0
        in_specs.append(pl.BlockSpec(
            (None, kl, tn), functools.partial(lambda j, i, layer, rb, cb: (layer, rb, cb + j), layer=layer, rb=rb,
                                              cb=co // tn)))
        operands.append(w)
        scratch.append(pltpu.VMEM((kl, tn), BF16))
    for arr, blk, imap in list(extras) + (list(extras_s) if side else []):
        in_specs.append(pl.BlockSpec(blk, imap))
        operands.append(arr)
    out_specs = [pl.BlockSpec((tm, tn), lambda j, i: (i, j))]
    out_shape = [jax.ShapeDtypeStruct((m, n_cols), out_dtype)]
    if side:
        out_specs.append(pl.BlockSpec((ms, tn), lambda j, i: (0, j)))
        out_shape.append(jax.ShapeDtypeStruct((ms, n_cols), out_dtype))
    kern = functools.partial(_mm_kernel, n_lhs=len(lhs), dots=[(d[0], d[5]) for d in dots],
                             n_extra=len(extras), epilogue=epilogue, side=bool(side))
    outs = pl.pallas_call(
        kern,
        grid=(n_cols // tn, m // tm),
        in_specs=in_specs,
        out_specs=out_specs,
        out_shape=out_shape,
        scratch_shapes=scratch,
        compiler_params=_params("parallel", "arbitrary"),
        name=name,
    )(*operands)
    return outs if side else outs[0]


def _ep_plain(accs, extras):
    return accs[0]


def _ep_glu(accs, extras):
    return accs[0] * jax.nn.sigmoid(accs[1])


def _ep_resid(accs, extras):
    return extras[0] + accs[0]


def _ep_ple(accs, extras):
    return extras[0] + jax.nn.sigmoid(accs[0]) * accs[1]


def _ep_rope(accs, extras):
    cos, sin_signed = extras
    y = accs[0]
    hd = cos.shape[1]
    outs = []
    for h in range(y.shape[1] // hd):
        xh = y[:, h * hd:(h + 1) * hd]
        outs.append(xh * cos + pltpu.roll(xh, hd // 2, 1) * sin_signed)
    return jnp.concatenate(outs, axis=1)


def _rope_tables(pos0, s, hd):
    inv = 1.0 / (ROPE_THETA ** (jnp.arange(0, hd, 2, dtype=F32) / hd))
    pos = (jnp.arange(s, dtype=jnp.int32) + pos0).astype(F32)
    ang = pos[:, None] * inv[None, :]
    cos, sin = jnp.cos(ang), jnp.sin(ang)
    return jnp.concatenate([cos, cos], axis=-1), jnp.concatenate([-sin, sin], axis=-1)


def _mlp_kernel(pt_ref, z_ref, h_ref, wu_ref, wd_ref, *refs, n_page, ppb):
    page_refs = refs[:n_page]
    o_ref = refs[n_page]
    f = pl.program_id(1)

    @pl.when(f == 0)
    def _():
        o_ref[...] = h_ref[...]

    hm = jnp.dot(z_ref[...], wu_ref[...], preferred_element_type=F32)
    hm = jnp.square(jnp.maximum(hm, 0.0)).astype(BF16)
    o_ref[...] += jnp.dot(hm, wd_ref[...], preferred_element_type=F32)

    if n_page:
        km_ref = refs[n_page + 1]
        ps = page_refs[0].shape[0]
        for r in range(n_page // ppb):
            s = jnp.sum(page_refs[r * ppb][...], axis=0)
            for e in range(1, ppb):
                s = s + jnp.sum(page_refs[r * ppb + e][...], axis=0)
            km_ref[r] = s * (1.0 / (ppb * ps))


def _mlp(z, h, w_up, w_down, paged=None):
    m, d = z.shape
    dff = w_up.shape[1]
    tm = _tile(m, 512)
    tf = _tile(dff, 512)
    nf = dff // tf
    steps = (m // tm) * nf
    once = pl.Buffered(1)
    in_specs = [
        pl.BlockSpec((tm, d), lambda i, f, pt: (i, 0), pipeline_mode=once),
        pl.BlockSpec((tm, d), lambda i, f, pt: (i, 0), pipeline_mode=once),
        pl.BlockSpec((d, tf), lambda i, f, pt: (0, f)),
        pl.BlockSpec((tf, d), lambda i, f, pt: (f, 0)),
    ]
    out_specs = [pl.BlockSpec((tm, d), lambda i, f, pt: (i, 0))]
    out_shape = [jax.ShapeDtypeStruct((m, d), F32)]
    operands = [z, h, w_up, w_down]
    n_page, ppb = 0, 1
    if paged is None:
        pt_flat = jnp.zeros((1,), jnp.int32)
    else:
        cache_k, layer, pt_flat, ppb = paged
        _, _, ps, n_heads, hd = cache_k.shape
        n_slots = pt_flat.shape[0]
        assert n_slots % (steps * ppb) == 0
        n_page = n_slots // steps
        for e in range(n_page):
            in_specs.append(pl.BlockSpec(
                (None, None, ps, n_heads, hd),
                functools.partial(lambda i, f, pt, e: (layer, pt[(i * nf + f) * n_page + e], 0, 0, 0), e=e)))
        operands += [cache_k] * n_page
        out_specs.append(pl.BlockSpec((n_page // ppb, n_heads, hd), lambda i, f, pt: (i * nf + f, 0, 0)))
        out_shape.append(jax.ShapeDtypeStruct((n_slots // ppb, n_heads, hd), F32))
    outs = pl.pallas_call(
        functools.partial(_mlp_kernel, n_page=n_page, ppb=ppb),
        grid_spec=pltpu.PrefetchScalarGridSpec(
            num_scalar_prefetch=1, grid=(m // tm, nf), in_specs=in_specs, out_specs=out_specs),
        out_shape=out_shape,
        compiler_params=_params("parallel", "arbitrary"),
        name="mlp",
    )(pt_flat, *operands)
    return outs[0] if paged is None else outs


def _ln_swish(y, g, beta):
    mu = jnp.mean(y, axis=-1, keepdims=True)
    xc = y - mu
    var = jnp.mean(xc * xc, axis=-1, keepdims=True)
    yn = xc * lax.rsqrt(var + NORM_EPS) * g + beta
    return yn * jax.nn.sigmoid(yn)


def _conv_prompt_kernel(cur_ref, halo_ref, w_ref, b_ref, g_ref, beta_ref, o_ref, ext_ref, sh_ref, y_ref, *, kw, cc):
    t = pl.program_id(1)
    tt, c = cur_ref.shape
    ext_ref[0:CONV_HALO, :] = jnp.where(t == 0, 0.0, halo_ref[...])
    ext_ref[CONV_HALO:CONV_HALO + tt, :] = cur_ref[...]
    off = CONV_HALO - (kw - 1)
    rows = sh_ref.shape[1]
    for phase in range(1, SUBLANES):
        sh_ref[phase - 1] = ext_ref[phase:phase + rows, :]
    groups = tt // SUBLANES
    for c0 in range(0, c, cc):
        acc = None
        for k in range(kw):
            phase = (off + k) % SUBLANES
            a0 = off + k - phase
            src = ext_ref if phase == 0 else sh_ref.at[phase - 1]
            term = src[a0:a0 + tt, c0:c0 + cc].reshape(groups, SUBLANES, cc) * w_ref[k, :, c0:c0 + cc][None]
            acc = term if acc is None else acc + term
        y_ref[:, c0:c0 + cc] = acc.reshape(tt, cc) + b_ref[:, c0:c0 + cc]
    o_ref[...] = _ln_swish(y_ref[...], g_ref[...], beta_ref[...]).astype(o_ref.dtype)


def _conv_prompt(u, batch, seq, w_dw, b_dw, ln_g, ln_b):
    m, c = u.shape
    kw = w_dw.shape[0]
    assert kw - 1 <= CONV_HALO
    tt = _tile(seq, 128)
    assert tt % CONV_HALO == 0
    nt = seq // tt
    hb = tt // CONV_HALO
    vec = lambda a: a.reshape(1, c)
    vspec = pl.BlockSpec((1, c), lambda b, t: (0, 0))
    kern = functools.partial(_conv_prompt_kernel, kw=kw, cc=min(LANES, c))
    w_rep = jnp.broadcast_to(w_dw[:, None, :], (kw, SUBLANES, c))
    return pl.pallas_call(
        kern,
        grid=(batch, nt),
        in_specs=[
            pl.BlockSpec((tt, c), lambda b, t: (b * nt + t, 0)),
            pl.BlockSpec((CONV_HALO, c), lambda b, t: (jnp.maximum((b * nt + t) * hb - 1, 0), 0)),
            pl.BlockSpec((kw, SUBLANES, c), lambda b, t: (0, 0, 0)),
            vspec, vspec, vspec,
        ],
        out_specs=pl.BlockSpec((tt, c), lambda b, t: (b * nt + t, 0)),
        out_shape=jax.ShapeDtypeStruct((m, c), BF16),
        scratch_shapes=[pltpu.VMEM((CONV_HALO + tt, c), F32),
                        pltpu.VMEM((SUBLANES - 1, CONV_HALO + tt - SUBLANES, c), F32),
                        pltpu.VMEM((tt, c), F32)],
        compiler_params=_params("parallel", "parallel"),
        name="conv_prompt",
    )(u, u, w_rep, vec(b_dw), vec(ln_g), vec(ln_b))


def _conv_sample_kernel(st_ref, u_ref, w_ref, b_ref, g_ref, beta_ref, o_ref, ns_ref):
    ks = st_ref.shape[0]
    st = st_ref[...]
    u = u_ref[...]
    y = jnp.sum(st * w_ref[0:ks, :], axis=0, keepdims=True) + u * w_ref[ks:ks + 1, :] + b_ref[...]
    o_ref[...] = _ln_swish(y, g_ref[...], beta_ref[...]).astype(o_ref.dtype)
    ns_ref[0:ks - 1, :] = st_ref[1:ks, :]
    ns_ref[ks - 1:ks, :] = u


def _conv_sample(u, state, w_dw, b_dw, ln_g, ln_b):
    db, c = u.shape
    ks = state.shape[1]
    vec = lambda a: a.reshape(1, c)
    vspec = pl.BlockSpec((1, c), lambda b: (0, 0))
    out, new_state = pl.pallas_call(
        _conv_sample_kernel,
        grid=(db,),
        in_specs=[
            pl.BlockSpec((None, ks, c), lambda b: (b, 0, 0)),
            pl.BlockSpec((None, 1, c), lambda b: (b, 0, 0)),
            pl.BlockSpec((ks + 1, c), lambda b: (0, 0)),
            vspec, vspec, vspec,
        ],
        out_specs=[pl.BlockSpec((None, 1, c), lambda b: (b, 0, 0)), pl.BlockSpec((None, ks, c), lambda b: (b, 0, 0))],
        out_shape=[jax.ShapeDtypeStruct((db, 1, c), F32), jax.ShapeDtypeStruct((db, ks, c), F32)],
        compiler_params=_params("parallel"),
        name="conv_sample",
    )(state, u.reshape(db, 1, c), w_dw, vec(b_dw), vec(ln_g), vec(ln_b))
    return out.reshape(db, c), new_state


def _attn_block_body(jj, q_ref, kb_ref, vt_ref, km_ref, o_ref, scale):
    blk = MOBA_BLOCK
    q = q_ref[jj * blk:(jj + 1) * blk, :]
    qb = q.astype(BF16)
    qs = (q * (scale * LOG2_E)).astype(BF16)
    ln = (jj + 1) * blk
    nt = (((1,), (1,)), ((), ()))
    st = lax.dot_general(kb_ref[0:ln, :], qs, nt, preferred_element_type=F32)
    pieces = []
    if jj > MOBA_TOPK:
        g = lax.dot_general(km_ref[...].astype(BF16), qb, nt, preferred_element_type=F32)
        for n in range(jj):
            cnt = jnp.zeros((1, blk), F32)
            for m in range(jj):
                if m == n:
                    continue
                beats = (g[m:m + 1] >= g[n:n + 1]) if m < n else (g[m:m + 1] > g[n:n + 1])
                cnt = cnt + jnp.where(beats, 1.0, 0.0)
            pieces.append(jnp.where(cnt < MOBA_TOPK, st[n * blk:(n + 1) * blk, :], -jnp.inf))
    else:
        pieces = [st[n * blk:(n + 1) * blk, :] for n in range(jj)]
    key = lax.broadcasted_iota(jnp.int32, (blk, blk), 0)
    qry = lax.broadcasted_iota(jnp.int32, (blk, blk), 1)
    pieces.append(jnp.where(key <= qry, st[jj * blk:ln, :], -jnp.inf))
    mx = jnp.max(pieces[-1], axis=0, keepdims=True)
    for pc in pieces[:-1]:
        mx = jnp.maximum(mx, jnp.max(pc, axis=0, keepdims=True))
    ps = [jnp.exp2(pc - mx) for pc in pieces]
    den = ps[0].sum(axis=0, keepdims=True)
    for p in ps[1:]:
        den = den + p.sum(axis=0, keepdims=True)
    pcat = jnp.concatenate(ps, axis=0) if len(ps) > 1 else ps[0]
    acc_t = jnp.dot(vt_ref[:, 0:ln], pcat.astype(BF16), preferred_element_type=F32)
    o_ref[jj * blk:(jj + 1) * blk, :] = (acc_t / den).T.astype(o_ref.dtype)


def _attn_prompt_kernel(q_ref, k_ref, v_ref, o_ref, kb_ref, vt_ref, km_ref, *, nb, scale):
    blk = MOBA_BLOCK
    kb_ref[...] = k_ref[...].astype(BF16)
    km_ref[...] = jnp.zeros(km_ref.shape, F32)
    for n in range(nb):
        vt_ref[:, n * blk:(n + 1) * blk] = v_ref[n * blk:(n + 1) * blk, :].T.astype(BF16)
        km_ref[n:n + 1, :] = jnp.mean(k_ref[n * blk:(n + 1) * blk, :], axis=0, keepdims=True)
    for jj in range(nb):
        _attn_block_body(jj, q_ref, kb_ref, vt_ref, km_ref, o_ref, scale)


def _attn_prompt(q, k, v, batch, seq, n_heads, hd):
    assert seq % MOBA_BLOCK == 0
    nb = seq // MOBA_BLOCK
    m = batch * seq
    bf16_rows = 2 * SUBLANES
    km_rows = -(-nb // bf16_rows) * bf16_rows
    kern = functools.partial(_attn_prompt_kernel, nb=nb, scale=hd ** -0.5)
    spec = pl.BlockSpec((seq, hd), lambda b, h: (b, h))
    return pl.pallas_call(
        kern,
        grid=(batch, n_heads),
        in_specs=[spec, spec, spec],
        out_specs=spec,
        out_shape=jax.ShapeDtypeStruct((m, n_heads * hd), BF16),
        scratch_shapes=[pltpu.VMEM((seq, hd), BF16), pltpu.VMEM((hd, seq), BF16), pltpu.VMEM((km_rows, hd), F32)],
        compiler_params=_params("parallel", "parallel"),
        name="attn_prompt",
    )(q, k, v)


def _topk_kernel(km_ref, q_ref, o_ref):
    nblk = km_ref.shape[0]
    gate = jnp.sum(km_ref[...] * q_ref[...][None], axis=-1, keepdims=True)
    blk_id = lax.broadcasted_iota(jnp.int32, gate.shape, 0)
    for s in range(MOBA_TOPK):
        mx = jnp.max(gate, axis=0, keepdims=True)
        idx = jnp.min(jnp.where(gate == mx, blk_id, nblk), axis=0, keepdims=True)
        o_ref[s:s + 1] = idx
        gate = jnp.where(blk_id == idx, -jnp.inf, gate)


def _sample_topk(kmean, q):
    db, nblk, n_heads, hd = kmean.shape
    assert nblk >= MOBA_TOPK
    out = pl.pallas_call(
        _topk_kernel,
        grid=(db,),
        in_specs=[pl.BlockSpec((None, nblk, n_heads, hd), lambda b: (b, 0, 0, 0)),
                  pl.BlockSpec((None, n_heads, hd), lambda b: (b, 0, 0))],
        out_specs=pl.BlockSpec((None, MOBA_TOPK, n_heads, 1), lambda b: (b, 0, 0, 0)),
        out_shape=jax.ShapeDtypeStruct((db, MOBA_TOPK, n_heads, 1), jnp.int32),
        compiler_params=_params("parallel"),
        name="sample_topk",
    )(kmean, q)
    return jnp.swapaxes(out.reshape(db, MOBA_TOPK, n_heads), 1, 2)


def _attn_sample_kernel(pt_ref, sel_ref, q_ref, kn_ref, vn_ref, ck_hbm, cv_hbm, o_ref, kbuf, vbuf, sems, *,
                        layer, n_pages, ppb, scale):
    b = pl.program_id(0)
    n_heads, hd = q_ref.shape
    n_slab = MOBA_TOPK * ppb

    def slab_copies(h, slot):
        cps = []
        for s in range(MOBA_TOPK):
            blk = sel_ref[(b * n_heads + h) * MOBA_TOPK + s]
            for e in range(ppb):
                page = pt_ref[b * n_pages + blk * ppb + e]
                i = s * ppb + e
                cps.append(pltpu.make_async_copy(ck_hbm.at[layer, page, :, h, :], kbuf.at[slot, i], sems.at[0, slot, i]))
                cps.append(pltpu.make_async_copy(cv_hbm.at[layer, page, :, h, :], vbuf.at[slot, i], sems.at[1, slot, i]))
        return cps

    for cp in slab_copies(0, 0):
        cp.start()
    for h in range(n_heads):
        slot = h % 2
        if h + 1 < n_heads:
            for cp in slab_copies(h + 1, 1 - slot):
                cp.start()
        for cp in slab_copies(h, slot):
            cp.wait()
        q = q_ref[h:h + 1, :]
        q8 = jnp.broadcast_to(q, (SUBLANES, hd)).astype(BF16)
        ss = [lax.dot_general(q8, kbuf[slot, i].astype(BF16), (((1,), (1,)), ((), ())),
                              preferred_element_type=F32) * scale for i in range(n_slab)]
        s_self = jnp.sum(q * kn_ref[h:h + 1, :], axis=1, keepdims=True) * scale
        mx = s_self
        for s in ss:
            mx = jnp.maximum(mx, jnp.max(s, axis=1, keepdims=True))
        p_self = jnp.exp(s_self - mx)
        den = p_self
        acc = p_self * vn_ref[h:h + 1, :]
        for i, s in enumerate(ss):
            p = jnp.exp(s - mx)
            den = den + jnp.sum(p, axis=1, keepdims=True)
            acc = acc + jnp.dot(p.astype(BF16), vbuf[slot, i].astype(BF16), preferred_element_type=F32)
        o_ref[h:h + 1, :] = (acc / den)[0:1, :]


def _attn_sample(q, k_new, v_new, cache_k, cache_v, layer, pt_flat, sel_flat, n_pages, ppb):
    db, n_heads, hd = q.shape
    ps = cache_k.shape[2]
    n_slab = MOBA_TOPK * ppb
    row_spec = pl.BlockSpec((None, n_heads, hd), lambda b, pt, sel: (b, 0, 0))
    any_spec = pl.BlockSpec(memory_space=pl.ANY)
    out = pl.pallas_call(
        functools.partial(_attn_sample_kernel, layer=layer, n_pages=n_pages, ppb=ppb, scale=hd ** -0.5),
        grid_spec=pltpu.PrefetchScalarGridSpec(
            num_scalar_prefetch=2,
            grid=(db,),
            in_specs=[row_spec, row_spec, row_spec, any_spec, any_spec],
            out_specs=row_spec,
            scratch_shapes=[pltpu.VMEM((2, n_slab, ps, hd), F32), pltpu.VMEM((2, n_slab, ps, hd), F32),
                            pltpu.SemaphoreType.DMA((2, 2, n_slab))],
        ),
        out_shape=jax.ShapeDtypeStruct((db, n_heads, hd), F32),
        compiler_params=_params("arbitrary"),
        name="attn_sample",
    )(pt_flat, sel_flat, q, k_new, v_new, cache_k, cache_v)
    return out.reshape(db, n_heads * hd)


def _in_projection(z, z_s, w_in, layer, c, a, rope_p, rope_s, table_rows, hd):
    m, ms = z.shape[0], z_s.shape[0]
    tm = _tile(m, 1024)
    tn_glu = _tile(c, 256)
    tn = _tile(a, 512)
    assert table_rows % tm == 0
    period = table_rows // tm
    tbl = [(t, (tm, hd), lambda j, i: (i % period, 0)) for t in rope_p]
    tbl_s = [(t, (ms, hd), lambda j, i: (0, 0)) for t in rope_s]
    col = lambda off, group=0: (0, w_in, layer, 0, off, group)
    u = _matmul([z], [col(0), col(c, 1)], [], _ep_glu, c, tn_glu, F32, "in_proj_glu", side=([z_s], []))
    q = _matmul([z], [col(2 * c)], tbl, _ep_rope, a, tn, F32, "in_proj_q", side=([z_s], tbl_s))
    k = _matmul([z], [col(2 * c + a)], tbl, _ep_rope, a, tn, F32, "in_proj_k", side=([z_s], tbl_s))
    v = _matmul([z], [col(2 * c + 2 * a)], [], _ep_plain, a, tn, F32, "in_proj_v", side=([z_s], []))
    return tuple(zip(u, q, k, v))


def _post_mixer(x, conv_out, attn, p, layer, w_out, g_ffn, w_up, w_down, g_ple, w_ple_gate, w_ple_proj, paged=None):
    m, d = x.shape
    c = conv_out.shape[1]
    assert attn.shape[1] == c
    tm = _tile(m, 1024)
    tn = _tile(d, 512)
    resid = lambda r: (r, (tm, tn), lambda j, i: (i, j))
    h = _matmul([conv_out, attn], [(0, w_out, layer, 0, 0, 0), (1, w_out, layer, 1, 0, 0)], [resid(x)], _ep_resid,
                d, tn, F32, "out_proj")
    h = _mlp(_rmsnorm(h, g_ffn, BF16), h, w_up, w_down, paged)
    if paged is not None:
        h, kmean = h
    z = _rmsnorm(h, g_ple, BF16)
    out = _matmul([z, p], [(0, w_ple_gate, layer, 0, 0, 0), (1, w_ple_proj, layer, 0, 0, 1)], [resid(h)], _ep_ple,
                  d, tn, F32, "ple")
    return out if paged is None else (out, kmean)


def kernel(x_prompt, x_sample, cache_k, cache_v, state_conv, page_table, p_prompt, p_sample, g_mix, w_in, w_dw, b_dw, g_conv_ln, b_conv_ln, w_out, g_ffn, w_up, w_down, g_ple, w_ple_gate, w_ple_proj, g_final):
    batch, seq, d = x_prompt.shape
    db, dseq, _ = x_sample.shape
    assert dseq == 1
    depth, n_pool, page, n_heads, hd = cache_k.shape
    n_pages = page_table.shape[1]
    past_len = n_pages * page
    assert MOBA_BLOCK % page == 0 and past_len % MOBA_BLOCK == 0
    ppb = MOBA_BLOCK // page
    c = w_dw.shape[-1]
    a = n_heads * hd
    assert c + a == d

    hp = x_prompt.reshape(batch * seq, d)
    hs = x_sample.reshape(db, d)
    cos_p, sin_p = _rope_tables(0, seq, hd)
    cos_s, sin_s = _rope_tables(past_len, 1, hd)
    cos_s = jnp.broadcast_to(cos_s, (db, hd))
    sin_s = jnp.broadcast_to(sin_s, (db, hd))
    pt_flat = page_table.reshape(-1).astype(jnp.int32)

    kp_l, vp_l, cp_l, ks_l, vs_l, cs_l = [], [], [], [], [], []
    for l in range(depth):
        post = (l, w_out, g_ffn[l], w_up[l].astype(BF16), w_down[l].astype(BF16), g_ple[l], w_ple_gate, w_ple_proj)
        conv_w = (w_dw[l], b_dw[l], g_conv_ln[l], b_conv_ln[l])

        z = _rmsnorm(hp, g_mix[l], BF16)
        z_s = _rmsnorm(hs, g_mix[l], BF16)
        (u, q, k, v), (u_s, q_s, k_s, v_s) = _in_projection(z, z_s, w_in, l, c, a, (cos_p, sin_p), (cos_s, sin_s),
                                                            seq, hd)

        conv_out = _conv_prompt(u, batch, seq, *conv_w)
        attn = _attn_prompt(q, k, v, batch, seq, n_heads, hd)
        hp, kmean = _post_mixer(hp, conv_out, attn, p_prompt[l].reshape(batch * seq, -1), *post,
                                paged=(cache_k, l, pt_flat, ppb))
        kp_l.append(k.reshape(batch, seq, n_heads, hd))
        vp_l.append(v.reshape(batch, seq, n_heads, hd))
        cp_l.append(u.reshape(batch, seq, c)[:, seq - (w_dw.shape[1] - 1):, :])

        conv_out, new_state = _conv_sample(u_s, state_conv[l], *conv_w)
        h3 = lambda t: t.reshape(db, n_heads, hd)
        sel = _sample_topk(kmean.reshape(db, n_pages // ppb, n_heads, hd), h3(q_s))
        attn = _attn_sample(h3(q_s), h3(k_s), h3(v_s), cache_k, cache_v, l, pt_flat, sel.reshape(-1), n_pages, ppb)
        hs = _post_mixer(hs, conv_out.astype(BF16), attn.astype(BF16), p_sample[l].reshape(db, -1), *post)
        ks_l.append(k_s.reshape(db, 1, n_heads, hd))
        vs_l.append(v_s.reshape(db, 1, n_heads, hd))
        cs_l.append(new_state)

    y_prompt = _rmsnorm(hp, g_final, F32).reshape(batch, seq, d)
    y_sample = _rmsnorm(hs, g_final, F32).reshape(db, 1, d)
    stack = lambda xs: xs[0][None] if len(xs) == 1 else jnp.stack(xs)
    return (y_prompt, y_sample, stack(kp_l), stack(vp_l), stack(cp_l), stack(ks_l), stack(vs_l), stack(cs_l))
```

```python
import functools

import jax
import jax.numpy as jnp
from jax import lax
from jax.experimental import pallas as pl
from jax.experimental.pallas import tpu as pltpu

F32 = jnp.float32
BF16 = jnp.bfloat16

MOBA_BLOCK = 256
MOBA_TOPK = 3
ROPE_THETA = 10000.0
NORM_EPS = 1e-6
LOG2_E = 1.4426950408889634
LANES = 128
SUBLANES = 8
CONV_HALO = 32
MM_ROW_CHUNK = 512
NORM_ROW_CHUNK = 64
VMEM_LIMIT = 60 * 1024 * 1024


def _params(*sem):
    return pltpu.CompilerParams(dimension_semantics=sem, vmem_limit_bytes=VMEM_LIMIT)


def _tile(n, pref):
    if n <= pref:
        return n
    t = pref
    while n % t:
        t //= 2
    return t


def _rmsnorm_kernel(x_ref, g_ref, o_ref):
    x = x_ref[...]
    ms = jnp.mean(x * x, axis=-1, keepdims=True)
    o_ref[...] = (x * lax.rsqrt(ms + NORM_EPS) * g_ref[...]).astype(o_ref.dtype)


def _rmsnorm(x, g, out_dtype):
    m, d = x.shape
    tm = _tile(m, 256)
    return pl.pallas_call(
        _rmsnorm_kernel,
        grid=(m // tm,),
        in_specs=[pl.BlockSpec((tm, d), lambda i: (i, 0)), pl.BlockSpec((1, d), lambda i: (0, 0))],
        out_specs=pl.BlockSpec((tm, d), lambda i: (i, 0)),
        out_shape=jax.ShapeDtypeStruct((m, d), out_dtype),
        compiler_params=_params("parallel"),
        name="rmsnorm",
    )(x, g.reshape(1, d))


def _mm_kernel(*refs, n_lhs, dots, n_extra, epilogue, side):
    n_dot = len(dots)
    groups = 2 if side else 1
    pos = 0
    lhs = [refs[pos + g * n_lhs:pos + (g + 1) * n_lhs] for g in range(groups)]
    pos += groups * n_lhs
    ws = refs[pos:pos + n_dot]
    pos += n_dot
    extras = [refs[pos + g * n_extra:pos + (g + 1) * n_extra] for g in range(groups)]
    pos += groups * n_extra
    outs = refs[pos:pos + groups]
    wbs = refs[pos + groups:pos + groups + n_dot]
    i = pl.program_id(1)

    @pl.when(i == 0)
    def _():
        for w, wb in zip(ws, wbs):
            wb[...] = w[...].astype(BF16)

    def run(g):
        rows = outs[g].shape[0]
        rc = MM_ROW_CHUNK if rows % MM_ROW_CHUNK == 0 else rows
        for r0 in range(0, rows, rc):
            accs = {}
            for (li, group), wb in zip(dots, wbs):
                a = lhs[g][li][r0:r0 + rc, :]
                if a.dtype != BF16:
                    a = a.astype(BF16)
                d = jnp.dot(a, wb[...], preferred_element_type=F32)
                accs[group] = d if group not in accs else accs[group] + d
            res = epilogue([accs[k] for k in sorted(accs)], [e[r0:r0 + rc, :] for e in extras[g]])
            outs[g][r0:r0 + rc, :] = res.astype(outs[g].dtype)

    run(0)
    if side:
        pl.when(i == pl.num_programs(1) - 1)(functools.partial(run, 1))


def _matmul(lhs, dots, extras, epilogue, n_cols, tn, out_dtype, name, side=None, tm_pref=1024):
    m = lhs[0].shape[0]
    tm = _tile(m, tm_pref)
    in_specs = [pl.BlockSpec((tm, a.shape[1]), lambda j, i: (i, 0)) for a in lhs]
    operands = list(lhs)
    if side:
        lhs_s, extras_s = side
        ms = lhs_s[0].shape[0]
        in_specs += [pl.BlockSpec((ms, a.shape[1]), lambda j, i: (0, 0)) for a in lhs_s]
        operands += list(lhs_s)
    scratch = []
    for li, w, layer, rb, co, _ in dots:
        kl = lhs[li].shape[1]
        assert co % tn == 0
        in_specs.append(pl.BlockSpec(
            (None, kl, tn), functools.partial(lambda j, i, layer, rb, cb: (layer, rb, cb + j), layer=layer, rb=rb,
                                              cb=co // tn)))
        operands.append(w)
        scratch.append(pltpu.VMEM((kl, tn), BF16))
    for arr, blk, imap in list(extras) + (list(extras_s) if side else []):
        in_specs.append(pl.BlockSpec(blk, imap))
        operands.append(arr)
    out_specs = [pl.BlockSpec((tm, tn), lambda j, i: (i, j))]
    out_shape = [jax.ShapeDtypeStruct((m, n_cols), out_dtype)]
    if side:
        out_specs.append(pl.BlockSpec((ms, tn), lambda j, i: (0, j)))
        out_shape.append(jax.ShapeDtypeStruct((ms, n_cols), out_dtype))
    kern = functools.partial(_mm_kernel, n_lhs=len(lhs), dots=[(d[0], d[5]) for d in dots],
                             n_extra=len(extras), epilogue=epilogue, side=bool(side))
    outs = pl.pallas_call(
        kern,
        grid=(n_cols // tn, m // tm),
        in_specs=in_specs,
        out_specs=out_specs,
        out_shape=out_shape,
        scratch_shapes=scratch,
        compiler_params=_params("parallel", "arbitrary"),
        name=name,
    )(*operands)
    return outs if side else outs[0]


def _ep_plain(accs, extras):
    return accs[0]


def _ep_glu(accs, extras):
    return accs[0] * jax.nn.sigmoid(accs[1])


def _ep_resid(accs, extras):
    return extras[0] + accs[0]


def _ep_rope(accs, extras):
    cos, sin_signed = extras
    y = accs[0]
    hd = cos.shape[1]
    outs = []
    for h in range(y.shape[1] // hd):
        xh = y[:, h * hd:(h + 1) * hd]
        outs.append(xh * cos + pltpu.roll(xh, hd // 2, 1) * sin_signed)
    return jnp.concatenate(outs, axis=1)


def _rope_tables(pos0, s, hd):
    inv = 1.0 / (ROPE_THETA ** (jnp.arange(0, hd, 2, dtype=F32) / hd))
    pos = (jnp.arange(s, dtype=jnp.int32) + pos0).astype(F32)
    ang = pos[:, None] * inv[None, :]
    cos, sin = jnp.cos(ang), jnp.sin(ang)
    return jnp.concatenate([cos, cos], axis=-1), jnp.concatenate([-sin, sin], axis=-1)


def _rms_scale(x, g):
    ms = jnp.mean(x * x, axis=-1, keepdims=True)
    return x * lax.rsqrt(ms + NORM_EPS) * g


def _mlp_kernel(pt_ref, h_ref, g_ref, wu_ref, wd_ref, *refs, n_page, ppb):
    page_refs = refs[:n_page]
    o_ref = refs[n_page]
    z_ref = refs[-1]
    f = pl.program_id(1)

    @pl.when(f == 0)
    def _():
        rows = h_ref.shape[0]
        rc = _tile(rows, NORM_ROW_CHUNK)
        for r0 in range(0, rows, rc):
            h = h_ref[r0:r0 + rc, :]
            o_ref[r0:r0 + rc, :] = h
            z_ref[r0:r0 + rc, :] = _rms_scale(h, g_ref[...]).astype(BF16)

    hm = jnp.dot(z_ref[...], wu_ref[...], preferred_element_type=F32)
    hm = jnp.square(jnp.maximum(hm, 0.0)).astype(BF16)
    o_ref[...] += jnp.dot(hm, wd_ref[...], preferred_element_type=F32)

    if n_page:
        km_ref = refs[n_page + 1]
        ps = page_refs[0].shape[0]
        for r in range(n_page // ppb):
            s = jnp.sum(page_refs[r * ppb][...], axis=0)
            for e in range(1, ppb):
                s = s + jnp.sum(page_refs[r * ppb + e][...], axis=0)
            km_ref[r] = s * (1.0 / (ppb * ps))


def _mlp(h, g, w_up, w_down, paged=None):
    m, d = h.shape
    dff = w_up.shape[1]
    tm = _tile(m, 512)
    tf = _tile(dff, 512)
    nf = dff // tf
    steps = (m // tm) * nf
    in_specs = [
        pl.BlockSpec((tm, d), lambda i, f, pt: (i, 0), pipeline_mode=pl.Buffered(1)),
        pl.BlockSpec((1, d), lambda i, f, pt: (0, 0)),
        pl.BlockSpec((d, tf), lambda i, f, pt: (0, f)),
        pl.BlockSpec((tf, d), lambda i, f, pt: (f, 0)),
    ]
    out_specs = [pl.BlockSpec((tm, d), lambda i, f, pt: (i, 0))]
    out_shape = [jax.ShapeDtypeStruct((m, d), F32)]
    operands = [h, g.reshape(1, d), w_up, w_down]
    n_page, ppb = 0, 1
    if paged is None:
        pt_flat = jnp.zeros((1,), jnp.int32)
    else:
        cache_k, layer, pt_flat, ppb = paged
        _, _, ps, n_heads, hd = cache_k.shape
        n_slots = pt_flat.shape[0]
        assert n_slots % (steps * ppb) == 0
        n_page = n_slots // steps
        for e in range(n_page):
            in_specs.append(pl.BlockSpec(
                (None, None, ps, n_heads, hd),
                functools.partial(lambda i, f, pt, e: (layer, pt[(i * nf + f) * n_page + e], 0, 0, 0), e=e)))
        operands += [cache_k] * n_page
        out_specs.append(pl.BlockSpec((n_page // ppb, n_heads, hd), lambda i, f, pt: (i * nf + f, 0, 0)))
        out_shape.append(jax.ShapeDtypeStruct((n_slots // ppb, n_heads, hd), F32))
    outs = pl.pallas_call(
        functools.partial(_mlp_kernel, n_page=n_page, ppb=ppb),
        grid_spec=pltpu.PrefetchScalarGridSpec(
            num_scalar_prefetch=1, grid=(m // tm, nf), in_specs=in_specs, out_specs=out_specs,
            scratch_shapes=[pltpu.VMEM((tm, d), BF16)]),
        out_shape=out_shape,
        compiler_params=_params("parallel", "arbitrary"),
        name="mlp",
    )(pt_flat, *operands)
    return outs[0] if paged is None else outs


def _ple_kernel(h_ref, p_ref, g_ref, gf_ref, wg_ref, wp_ref, y_ref, z_ref, pb_ref, *, final_norm):
    j = pl.program_id(1)
    tn = wg_ref.shape[1]

    rows = h_ref.shape[0]
    rc = _tile(rows, NORM_ROW_CHUNK)

    @pl.when(j == 0)
    def _():
        for r0 in range(0, rows, rc):
            z_ref[r0:r0 + rc, :] = _rms_scale(h_ref[r0:r0 + rc, :], g_ref[...]).astype(BF16)
        pb_ref[...] = p_ref[...].astype(BF16)

    cols = pl.ds(pl.multiple_of(j * tn, tn), tn)
    gate = jnp.dot(z_ref[...], wg_ref[...], preferred_element_type=F32)
    emb = jnp.dot(pb_ref[...], wp_ref[...].astype(BF16), preferred_element_type=F32)
    y_ref[:, cols] = h_ref[:, cols] + jax.nn.sigmoid(gate) * emb

    if final_norm:
        @pl.when(j == pl.num_programs(1) - 1)
        def _():
            for r0 in range(0, rows, rc):
                y_ref[r0:r0 + rc, :] = _rms_scale(y_ref[r0:r0 + rc, :], gf_ref[...])


def _ple(h, p, g_ple, g_final, w_gate, w_proj, layer, final_norm):
    m, d = h.shape
    pdim = p.shape[1]
    tm = _tile(m, 512)
    tn = _tile(d, 1024)
    vec = pl.BlockSpec((1, d), lambda i, j: (0, 0))
    return pl.pallas_call(
        functools.partial(_ple_kernel, final_norm=final_norm),
        grid=(m // tm, d // tn),
        in_specs=[
            pl.BlockSpec((tm, d), lambda i, j: (i, 0), pipeline_mode=pl.Buffered(1)),
            pl.BlockSpec((tm, pdim), lambda i, j: (i, 0)),
            vec, vec,
            pl.BlockSpec((d, tn), lambda i, j: (0, j)),
            pl.BlockSpec((None, pdim, tn), lambda i, j: (layer, 0, j)),
        ],
        out_specs=pl.BlockSpec((tm, d), lambda i, j: (i, 0)),
        out_shape=jax.ShapeDtypeStruct((m, d), F32),
        scratch_shapes=[pltpu.VMEM((tm, d), BF16), pltpu.VMEM((tm, pdim), BF16)],
        compiler_params=_params("parallel", "arbitrary"),
        name="ple",
    )(h, p, g_ple.reshape(1, d), g_final.reshape(1, d), w_gate, w_proj)


def _ln_swish(y, g, beta):
    mu = jnp.mean(y, axis=-1, keepdims=True)
    xc = y - mu
    var = jnp.mean(xc * xc, axis=-1, keepdims=True)
    yn = xc * lax.rsqrt(var + NORM_EPS) * g + beta
    return yn * jax.nn.sigmoid(yn)


def _conv_prompt_kernel(cur_ref, halo_ref, w_ref, b_ref, g_ref, beta_ref, o_ref, ext_ref, sh_ref, y_ref, *, kw, cc):
    t = pl.program_id(1)
    tt, c = cur_ref.shape
    ext_ref[0:CONV_HALO, :] = jnp.where(t == 0, 0.0, halo_ref[...])
    ext_ref[CONV_HALO:CONV_HALO + tt, :] = cur_ref[...]
    off = CONV_HALO - (kw - 1)
    rows = sh_ref.shape[1]
    for phase in range(1, SUBLANES):
        sh_ref[phase - 1] = ext_ref[phase:phase + rows, :]
    groups = tt // SUBLANES
    for c0 in range(0, c, cc):
        acc = None
        for k in range(kw):
            phase = (off + k) % SUBLANES
            a0 = off + k - phase
            src = ext_ref if phase == 0 else sh_ref.at[phase - 1]
            term = src[a0:a0 + tt, c0:c0 + cc].reshape(groups, SUBLANES, cc) * w_ref[k, :, c0:c0 + cc][None]
            acc = term if acc is None else acc + term
        y_ref[:, c0:c0 + cc] = acc.reshape(tt, cc) + b_ref[:, c0:c0 + cc]
    o_ref[...] = _ln_swish(y_ref[...], g_ref[...], beta_ref[...]).astype(o_ref.dtype)


def _conv_prompt(u, batch, seq, w_dw, b_dw, ln_g, ln_b):
    m, c = u.shape
    kw = w_dw.shape[0]
    assert kw - 1 <= CONV_HALO
    tt = _tile(seq, 128)
    assert tt % CONV_HALO == 0
    nt = seq // tt
    hb = tt // CONV_HALO
    vec = lambda a: a.reshape(1, c)
    vspec = pl.BlockSpec((1, c), lambda b, t: (0, 0))
    kern = functools.partial(_conv_prompt_kernel, kw=kw, cc=min(LANES, c))
    w_rep = jnp.broadcast_to(w_dw[:, None, :], (kw, SUBLANES, c))
    return pl.pallas_call(
        kern,
        grid=(batch, nt),
        in_specs=[
            pl.BlockSpec((tt, c), lambda b, t: (b * nt + t, 0)),
            pl.BlockSpec((CONV_HALO, c), lambda b, t: (jnp.maximum((b * nt + t) * hb - 1, 0), 0)),
            pl.BlockSpec((kw, SUBLANES, c), lambda b, t: (0, 0, 0)),
            vspec, vspec, vspec,
        ],
        out_specs=pl.BlockSpec((tt, c), lambda b, t: (b * nt + t, 0)),
        out_shape=jax.ShapeDtypeStruct((m, c), BF16),
        scratch_shapes=[pltpu.VMEM((CONV_HALO + tt, c), F32),
                        pltpu.VMEM((SUBLANES - 1, CONV_HALO + tt - SUBLANES, c), F32),
                        pltpu.VMEM((tt, c), F32)],
        compiler_params=_params("parallel", "parallel"),
        name="conv_prompt",
    )(u, u, w_rep, vec(b_dw), vec(ln_g), vec(ln_b))


def _conv_sample_kernel(st_ref, u_ref, w_ref, b_ref, g_ref, beta_ref, o_ref, ns_ref):
    ks = st_ref.shape[0]
    st = st_ref[...]
    u = u_ref[...]
    y = jnp.sum(st * w_ref[0:ks, :], axis=0, keepdims=True) + u * w_ref[ks:ks + 1, :] + b_ref[...]
    o_ref[...] = _ln_swish(y, g_ref[...], beta_ref[...]).astype(o_ref.dtype)
    ns_ref[0:ks - 1, :] = st_ref[1:ks, :]
    ns_ref[ks - 1:ks, :] = u


def _conv_sample(u, state, w_dw, b_dw, ln_g, ln_b):
    db, c = u.shape
    ks = state.shape[1]
    vec = lambda a: a.reshape(1, c)
    vspec = pl.BlockSpec((1, c), lambda b: (0, 0))
    out, new_state = pl.pallas_call(
        _conv_sample_kernel,
        grid=(db,),
        in_specs=[
            pl.BlockSpec((None, ks, c), lambda b: (b, 0, 0)),
            pl.BlockSpec((None, 1, c), lambda b: (b, 0, 0)),
            pl.BlockSpec((ks + 1, c), lambda b: (0, 0)),
            vspec, vspec, vspec,
        ],
        out_specs=[pl.BlockSpec((None, 1, c), lambda b: (b, 0, 0)), pl.BlockSpec((None, ks, c), lambda b: (b, 0, 0))],
        out_shape=[jax.ShapeDtypeStruct((db, 1, c), F32), jax.ShapeDtypeStruct((db, ks, c), F32)],
        compiler_params=_params("parallel"),
        name="conv_sample",
    )(state, u.reshape(db, 1, c), w_dw, vec(b_dw), vec(ln_g), vec(ln_b))
    return out.reshape(db, c), new_state


def _attn_block_body(jj, q_ref, kb_ref, vt_ref, km_ref, o_ref, scale):
    blk = MOBA_BLOCK
    q = q_ref[jj * blk:(jj + 1) * blk, :]
    qb = q.astype(BF16)
    qs = (q * (scale * LOG2_E)).astype(BF16)
    ln = (jj + 1) * blk
    nt = (((1,), (1,)), ((), ()))
    st = lax.dot_general(kb_ref[0:ln, :], qs, nt, preferred_element_type=F32)
    pieces = []
    if jj > MOBA_TOPK:
        g = lax.dot_general(km_ref[...].astype(BF16), qb, nt, preferred_element_type=F32)
        for n in range(jj):
            cnt = jnp.zeros((1, blk), F32)
            for m in range(jj):
                if m == n:
                    continue
                beats = (g[m:m + 1] >= g[n:n + 1]) if m < n else (g[m:m + 1] > g[n:n + 1])
                cnt = cnt + jnp.where(beats, 1.0, 0.0)
            pieces.append(jnp.where(cnt < MOBA_TOPK, st[n * blk:(n + 1) * blk, :], -jnp.inf))
    else:
        pieces = [st[n * blk:(n + 1) * blk, :] for n in range(jj)]
    key = lax.broadcasted_iota(jnp.int32, (blk, blk), 0)
    qry = lax.broadcasted_iota(jnp.int32, (blk, blk), 1)
    pieces.append(jnp.where(key <= qry, st[jj * blk:ln, :], -jnp.inf))
    mx = jnp.max(pieces[-1], axis=0, keepdims=True)
    for pc in pieces[:-1]:
        mx = jnp.maximum(mx, jnp.max(pc, axis=0, keepdims=True))
    ps = [jnp.exp2(pc - mx) for pc in pieces]
    den = ps[0].sum(axis=0, keepdims=True)
    for p in ps[1:]:
        den = den + p.sum(axis=0, keepdims=True)
    pcat = jnp.concatenate(ps, axis=0) if len(ps) > 1 else ps[0]
    acc_t = jnp.dot(vt_ref[:, 0:ln], pcat.astype(BF16), preferred_element_type=F32)
    o_ref[jj * blk:(jj + 1) * blk, :] = (acc_t / den).T.astype(o_ref.dtype)


def _attn_prompt_kernel(q_ref, k_ref, v_ref, o_ref, kb_ref, vt_ref, km_ref, *, nb, scale):
    blk = MOBA_BLOCK
    kb_ref[...] = k_ref[...].astype(BF16)
    km_ref[...] = jnp.zeros(km_ref.shape, F32)
    for n in range(nb):
        vt_ref[:, n * blk:(n + 1) * blk] = v_ref[n * blk:(n + 1) * blk, :].T.astype(BF16)
        km_ref[n:n + 1, :] = jnp.mean(k_ref[n * blk:(n + 1) * blk, :], axis=0, keepdims=True)
    for jj in range(nb):
        _attn_block_body(jj, q_ref, kb_ref, vt_ref, km_ref, o_ref, scale)


def _attn_prompt(q, k, v, batch, seq, n_heads, hd):
    assert seq % MOBA_BLOCK == 0
    nb = seq // MOBA_BLOCK
    m = batch * seq
    bf16_rows = 2 * SUBLANES
    km_rows = -(-nb // bf16_rows) * bf16_rows
    kern = functools.partial(_attn_prompt_kernel, nb=nb, scale=hd ** -0.5)
    spec = pl.BlockSpec((seq, hd), lambda b, h: (b, h))
    return pl.pallas_call(
        kern,
        grid=(batch, n_heads),
        in_specs=[spec, spec, spec],
        out_specs=spec,
        out_shape=jax.ShapeDtypeStruct((m, n_heads * hd), BF16),
        scratch_shapes=[pltpu.VMEM((seq, hd), BF16), pltpu.VMEM((hd, seq), BF16), pltpu.VMEM((km_rows, hd), F32)],
        compiler_params=_params("parallel", "parallel"),
        name="attn_prompt",
    )(q, k, v)


def _topk_kernel(km_ref, q_ref, o_ref):
    nblk = km_ref.shape[0]
    gate = jnp.sum(km_ref[...] * q_ref[...][None], axis=-1, keepdims=True)
    blk_id = lax.broadcasted_iota(jnp.int32, gate.shape, 0)
    for s in range(MOBA_TOPK):
        mx = jnp.max(gate, axis=0, keepdims=True)
        idx = jnp.min(jnp.where(gate == mx, blk_id, nblk), axis=0, keepdims=True)
        o_ref[s:s + 1] = idx
        gate = jnp.where(blk_id == idx, -jnp.inf, gate)


def _sample_topk(kmean, q):
    db, nblk, n_heads, hd = kmean.shape
    assert nblk >= MOBA_TOPK
    out = pl.pallas_call(
        _topk_kernel,
        grid=(db,),
        in_specs=[pl.BlockSpec((None, nblk, n_heads, hd), lambda b: (b, 0, 0, 0)),
                  pl.BlockSpec((None, n_heads, hd), lambda b: (b, 0, 0))],
        out_specs=pl.BlockSpec((None, MOBA_TOPK, n_heads, 1), lambda b: (b, 0, 0, 0)),
        out_shape=jax.ShapeDtypeStruct((db, MOBA_TOPK, n_heads, 1), jnp.int32),
        compiler_params=_params("parallel"),
        name="sample_topk",
    )(kmean, q)
    return jnp.swapaxes(out.reshape(db, MOBA_TOPK, n_heads), 1, 2)


def _attn_sample_kernel(pt_ref, sel_ref, q_ref, kn_ref, vn_ref, ck_hbm, cv_hbm, o_ref, kbuf, vbuf, sems, *,
                        layer, n_pages, ppb, scale):
    b = pl.program_id(0)
    n_heads, hd = q_ref.shape
    n_slab = MOBA_TOPK * ppb

    def slab_copies(h, slot):
        cps = []
        for s in range(MOBA_TOPK):
            blk = sel_ref[(b * n_heads + h) * MOBA_TOPK + s]
            for e in range(ppb):
                page = pt_ref[b * n_pages + blk * ppb + e]
                i = s * ppb + e
                cps.append(pltpu.make_async_copy(ck_hbm.at[layer, page, :, h, :], kbuf.at[slot, i], sems.at[0, slot, i]))
                cps.append(pltpu.make_async_copy(cv_hbm.at[layer, page, :, h, :], vbuf.at[slot, i], sems.at[1, slot, i]))
        return cps

    for cp in slab_copies(0, 0):
        cp.start()
    for h in range(n_heads):
        slot = h % 2
        if h + 1 < n_heads:
            for cp in slab_copies(h + 1, 1 - slot):
                cp.start()
        for cp in slab_copies(h, slot):
            cp.wait()
        q = q_ref[h:h + 1, :]
        q8 = jnp.broadcast_to(q, (SUBLANES, hd)).astype(BF16)
        ss = [lax.dot_general(q8, kbuf[slot, i].astype(BF16), (((1,), (1,)), ((), ())),
                              preferred_element_type=F32) * scale for i in range(n_slab)]
        s_self = jnp.sum(q * kn_ref[h:h + 1, :], axis=1, keepdims=True) * scale
        mx = s_self
        for s in ss:
            mx = jnp.maximum(mx, jnp.max(s, axis=1, keepdims=True))
        p_self = jnp.exp(s_self - mx)
        den = p_self
        acc = p_self * vn_ref[h:h + 1, :]
        for i, s in enumerate(ss):
            p = jnp.exp(s - mx)
            den = den + jnp.sum(p, axis=1, keepdims=True)
            acc = acc + jnp.dot(p.astype(BF16), vbuf[slot, i].astype(BF16), preferred_element_type=F32)
        o_ref[h:h + 1, :] = (acc / den)[0:1, :]


def _attn_sample(q, k_new, v_new, cache_k, cache_v, layer, pt_flat, sel_flat, n_pages, ppb):
    db, n_heads, hd = q.shape
    ps = cache_k.shape[2]
    n_slab = MOBA_TOPK * ppb
    row_spec = pl.BlockSpec((None, n_heads, hd), lambda b, pt, sel: (b, 0, 0))
    any_spec = pl.BlockSpec(memory_space=pl.ANY)
    out = pl.pallas_call(
        functools.partial(_attn_sample_kernel, layer=layer, n_pages=n_pages, ppb=ppb, scale=hd ** -0.5),
        grid_spec=pltpu.PrefetchScalarGridSpec(
            num_scalar_prefetch=2,
            grid=(db,),
            in_specs=[row_spec, row_spec, row_spec, any_spec, any_spec],
            out_specs=row_spec,
            scratch_shapes=[pltpu.VMEM((2, n_slab, ps, hd), F32), pltpu.VMEM((2, n_slab, ps, hd), F32),
                            pltpu.SemaphoreType.DMA((2, 2, n_slab))],
        ),
        out_shape=jax.ShapeDtypeStruct((db, n_heads, hd), F32),
        compiler_params=_params("arbitrary"),
        name="attn_sample",
    )(pt_flat, sel_flat, q, k_new, v_new, cache_k, cache_v)
    return out.reshape(db, n_heads * hd)


def _in_projection(z, z_s, w_in, layer, c, a, rope_p, rope_s, table_rows, hd):
    m, ms = z.shape[0], z_s.shape[0]
    tm = _tile(m, 1024)
    tn_glu = _tile(c, 256)
    tn = _tile(a, 512)
    assert table_rows % tm == 0
    period = table_rows // tm
    tbl = [(t, (tm, hd), lambda j, i: (i % period, 0)) for t in rope_p]
    tbl_s = [(t, (ms, hd), lambda j, i: (0, 0)) for t in rope_s]
    col = lambda off, group=0: (0, w_in, layer, 0, off, group)
    u = _matmul([z], [col(0), col(c, 1)], [], _ep_glu, c, tn_glu, F32, "in_proj_glu", side=([z_s], []))
    q = _matmul([z], [col(2 * c)], tbl, _ep_rope, a, tn, F32, "in_proj_q", side=([z_s], tbl_s))
    k = _matmul([z], [col(2 * c + a)], tbl, _ep_rope, a, tn, F32, "in_proj_k", side=([z_s], tbl_s))
    v = _matmul([z], [col(2 * c + 2 * a)], [], _ep_plain, a, tn, F32, "in_proj_v", side=([z_s], []))
    return tuple(zip(u, q, k, v))


def _post_mixer(x, conv_out, attn, p, layer, w_out, g_ffn, w_up, w_down, g_ple, w_ple_gate, w_ple_proj, g_final,
                final_norm, paged=None):
    m, d = x.shape
    c = conv_out.shape[1]
    assert attn.shape[1] == c
    tm = _tile(m, 1024)
    tn = _tile(d, 512)
    resid = lambda r: (r, (tm, tn), lambda j, i: (i, j))
    h = _matmul([conv_out, attn], [(0, w_out, layer, 0, 0, 0), (1, w_out, layer, 1, 0, 0)], [resid(x)], _ep_resid,
                d, tn, F32, "out_proj")
    h = _mlp(h, g_ffn, w_up, w_down, paged)
    if paged is not None:
        h, kmean = h
    out = _ple(h, p, g_ple, g_final, w_ple_gate, w_ple_proj, layer, final_norm)
    return out if paged is None else (out, kmean)


def kernel(x_prompt, x_sample, cache_k, cache_v, state_conv, page_table, p_prompt, p_sample, g_mix, w_in, w_dw, b_dw, g_conv_ln, b_conv_ln, w_out, g_ffn, w_up, w_down, g_ple, w_ple_gate, w_ple_proj, g_final):
    batch, seq, d = x_prompt.shape
    db, dseq, _ = x_sample.shape
    assert dseq == 1
    depth, n_pool, page, n_heads, hd = cache_k.shape
    n_pages = page_table.shape[1]
    past_len = n_pages * page
    assert MOBA_BLOCK % page == 0 and past_len % MOBA_BLOCK == 0
    ppb = MOBA_BLOCK // page
    c = w_dw.shape[-1]
    a = n_heads * hd
    assert c + a == d

    hp = x_prompt.reshape(batch * seq, d)
    hs = x_sample.reshape(db, d)
    cos_p, sin_p = _rope_tables(0, seq, hd)
    cos_s, sin_s = _rope_tables(past_len, 1, hd)
    cos_s = jnp.broadcast_to(cos_s, (db, hd))
    sin_s = jnp.broadcast_to(sin_s, (db, hd))
    pt_flat = page_table.reshape(-1).astype(jnp.int32)

    kp_l, vp_l, cp_l, ks_l, vs_l, cs_l = [], [], [], [], [], []
    for l in range(depth):
        post = (l, w_out, g_ffn[l], w_up[l].astype(BF16), w_down[l].astype(BF16), g_ple[l],
                w_ple_gate[l].astype(BF16), w_ple_proj, g_final, l == depth - 1)
        conv_w = (w_dw[l], b_dw[l], g_conv_ln[l], b_conv_ln[l])

        z = _rmsnorm(hp, g_mix[l], BF16)
        z_s = _rmsnorm(hs, g_mix[l], BF16)
        (u, q, k, v), (u_s, q_s, k_s, v_s) = _in_projection(z, z_s, w_in, l, c, a, (cos_p, sin_p), (cos_s, sin_s),
                                                            seq, hd)

        conv_out = _conv_prompt(u, batch, seq, *conv_w)
        attn = _attn_prompt(q, k, v, batch, seq, n_heads, hd)
        hp, kmean = _post_mixer(hp, conv_out, attn, p_prompt[l].reshape(batch * seq, -1), *post,
                                paged=(cache_k, l, pt_flat, ppb))
        kp_l.append(k.reshape(batch, seq, n_heads, hd))
        vp_l.append(v.reshape(batch, seq, n_heads, hd))
        cp_l.append(u.reshape(batch, seq, c)[:, seq - (w_dw.shape[1] - 1):, :])

        conv_out, new_state = _conv_sample(u_s, state_conv[l], *conv_w)
        h3 = lambda t: t.reshape(db, n_heads, hd)
        sel = _sample_topk(kmean.reshape(db, n_pages // ppb, n_heads, hd), h3(q_s))
        attn = _attn_sample(h3(q_s), h3(k_s), h3(v_s), cache_k, cache_v, l, pt_flat, sel.reshape(-1), n_pages, ppb)
        hs = _post_mixer(hs, conv_out.astype(BF16), attn.astype(BF16), p_sample[l].reshape(db, -1), *post)
        ks_l.append(k_s.reshape(db, 1, n_heads, hd))
        vs_l.append(v_s.reshape(db, 1, n_heads, hd))
        cs_l.append(new_state)

    y_prompt = hp.reshape(batch, seq, d)
    y_sample = hs.reshape(db, 1, d)
    stack = lambda xs: xs[0][None] if len(xs) == 1 else jnp.stack(xs)
    return (y_prompt, y_sample, stack(kp_l), stack(vp_l), stack(cp_l), stack(ks_l), stack(vs_l), stack(cs_l))
```

```python
import functools

import jax
import jax.numpy as jnp
from jax import lax
from jax.experimental import pallas as pl
from jax.experimental.pallas import tpu as pltpu

F32 = jnp.float32
BF16 = jnp.bfloat16

MOBA_BLOCK = 256
MOBA_TOPK = 3
ROPE_THETA = 10000.0
NORM_EPS = 1e-6
LOG2_E = 1.4426950408889634
LANES = 128
SUBLANES = 8
CONV_HALO = 32
MM_ROW_CHUNK = 512
NORM_ROW_CHUNK = 64
SAMPLE_HEAD_GROUP = 2
SAMPLE_HEAD_SLOTS = 6
VMEM_LIMIT = 60 * 1024 * 1024


def _params(*sem):
    return pltpu.CompilerParams(dimension_semantics=sem, vmem_limit_bytes=VMEM_LIMIT)


def _tile(n, pref):
    if n <= pref:
        return n
    t = pref
    while n % t:
        t //= 2
    return t


def _rmsnorm_kernel(x_ref, g_ref, o_ref):
    x = x_ref[...]
    ms = jnp.mean(x * x, axis=-1, keepdims=True)
    o_ref[...] = (x * lax.rsqrt(ms + NORM_EPS) * g_ref[...]).astype(o_ref.dtype)


def _rmsnorm(x, g, out_dtype):
    m, d = x.shape
    tm = _tile(m, 256)
    return pl.pallas_call(
        _rmsnorm_kernel,
        grid=(m // tm,),
        in_specs=[pl.BlockSpec((tm, d), lambda i: (i, 0)), pl.BlockSpec((1, d), lambda i: (0, 0))],
        out_specs=pl.BlockSpec((tm, d), lambda i: (i, 0)),
        out_shape=jax.ShapeDtypeStruct((m, d), out_dtype),
        compiler_params=_params("parallel"),
        name="rmsnorm",
    )(x, g.reshape(1, d))


def _mm_kernel(*refs, n_lhs, dots, n_extra, epilogue, side):
    n_dot = len(dots)
    groups = 2 if side else 1
    pos = 0
    lhs = [refs[pos + g * n_lhs:pos + (g + 1) * n_lhs] for g in range(groups)]
    pos += groups * n_lhs
    ws = refs[pos:pos + n_dot]
    pos += n_dot
    extras = [refs[pos + g * n_extra:pos + (g + 1) * n_extra] for g in range(groups)]
    pos += groups * n_extra
    outs = refs[pos:pos + groups]
    wbs = refs[pos + groups:pos + groups + n_dot]
    i = pl.program_id(1)

    @pl.when(i == 0)
    def _():
        for w, wb in zip(ws, wbs):
            wb[...] = w[...].astype(BF16)

    def run(g):
        rows = outs[g].shape[0]
        rc = MM_ROW_CHUNK if rows % MM_ROW_CHUNK == 0 else rows
        for r0 in range(0, rows, rc):
            accs = {}
            for (li, group), wb in zip(dots, wbs):
                a = lhs[g][li][r0:r0 + rc, :]
                if a.dtype != BF16:
                    a = a.astype(BF16)
                d = jnp.dot(a, wb[...], preferred_element_type=F32)
                accs[group] = d if group not in accs else accs[group] + d
            res = epilogue([accs[k] for k in sorted(accs)], [e[r0:r0 + rc, :] for e in extras[g]])
            outs[g][r0:r0 + rc, :] = res.astype(outs[g].dtype)

    run(0)
    if side:
        pl.when(i == pl.num_programs(1) - 1)(functools.partial(run, 1))


def _matmul(lhs, dots, extras, epilogue, n_cols, tn, out_dtype, name, side=None, tm_pref=1024):
    m = lhs[0].shape[0]
    tm = _tile(m, tm_pref)
    in_specs = [pl.BlockSpec((tm, a.shape[1]), lambda j, i: (i, 0)) for a in lhs]
    operands = list(lhs)
    if side:
        lhs_s, extras_s = side
        ms = lhs_s[0].shape[0]
        in_specs += [pl.BlockSpec((ms, a.shape[1]), lambda j, i: (0, 0)) for a in lhs_s]
        operands += list(lhs_s)
    scratch = []
    for li, w, layer, rb, co, _ in dots:
        kl = lhs[li].shape[1]
        assert co % tn == 0
        in_specs.append(pl.BlockSpec(
            (None, kl, tn), functools.partial(lambda j, i, layer, rb, cb: (layer, rb, cb + j), layer=layer, rb=rb,
                                              cb=co // tn)))
        operands.append(w)
        scratch.append(pltpu.VMEM((kl, tn), BF16))
    for arr, blk, imap in list(extras) + (list(extras_s) if side else []):
        in_specs.append(pl.BlockSpec(blk, imap))
        operands.append(arr)
    out_specs = [pl.BlockSpec((tm, tn), lambda j, i: (i, j))]
    out_shape = [jax.ShapeDtypeStruct((m, n_cols), out_dtype)]
    if side:
        out_specs.append(pl.BlockSpec((ms, tn), lambda j, i: (0, j)))
        out_shape.append(jax.ShapeDtypeStruct((ms, n_cols), out_dtype))
    kern = functools.partial(_mm_kernel, n_lhs=len(lhs), dots=[(d[0], d[5]) for d in dots],
                             n_extra=len(extras), epilogue=epilogue, side=bool(side))
    outs = pl.pallas_call(
        kern,
        grid=(n_cols // tn, m // tm),
        in_specs=in_specs,
        out_specs=out_specs,
        out_shape=out_shape,
        scratch_shapes=scratch,
        compiler_params=_params("parallel", "arbitrary"),
        name=name,
    )(*operands)
    return outs if side else outs[0]


def _ep_plain(accs, extras):
    return accs[0]


def _ep_glu(accs, extras):
    return accs[0] * jax.nn.sigmoid(accs[1])


def _ep_resid(accs, extras):
    return extras[0] + accs[0]


def _ep_rope(accs, extras):
    cos, sin_signed = extras
    y = accs[0]
    hd = cos.shape[1]
    outs = []
    for h in range(y.shape[1] // hd):
        xh = y[:, h * hd:(h + 1) * hd]
        outs.append(xh * cos + pltpu.roll(xh, hd // 2, 1) * sin_signed)
    return jnp.concatenate(outs, axis=1)


def _rope_tables(pos0, s, hd):
    inv = 1.0 / (ROPE_THETA ** (jnp.arange(0, hd, 2, dtype=F32) / hd))
    pos = (jnp.arange(s, dtype=jnp.int32) + pos0).astype(F32)
    ang = pos[:, None] * inv[None, :]
    cos, sin = jnp.cos(ang), jnp.sin(ang)
    return jnp.concatenate([cos, cos], axis=-1), jnp.concatenate([-sin, sin], axis=-1)


def _rms_scale(x, g):
    ms = jnp.mean(x * x, axis=-1, keepdims=True)
    return x * lax.rsqrt(ms + NORM_EPS) * g


def _mlp_kernel(pt_ref, h_ref, g_ref, wu_ref, wd_ref, *refs, n_page, ppb):
    page_refs = refs[:n_page]
    o_ref = refs[n_page]
    z_ref = refs[-1]
    f = pl.program_id(1)

    @pl.when(f == 0)
    def _():
        rows = h_ref.shape[0]
        rc = _tile(rows, NORM_ROW_CHUNK)
        for r0 in range(0, rows, rc):
            h = h_ref[r0:r0 + rc, :]
            o_ref[r0:r0 + rc, :] = h
            z_ref[r0:r0 + rc, :] = _rms_scale(h, g_ref[...]).astype(BF16)

    hm = jnp.dot(z_ref[...], wu_ref[...], preferred_element_type=F32)
    hm = jnp.square(jnp.maximum(hm, 0.0)).astype(BF16)
    o_ref[...] += jnp.dot(hm, wd_ref[...], preferred_element_type=F32)

    if n_page:
        km_ref = refs[n_page + 1]
        ps = page_refs[0].shape[0]
        for r in range(n_page // ppb):
            s = jnp.sum(page_refs[r * ppb][...], axis=0)
            for e in range(1, ppb):
                s = s + jnp.sum(page_refs[r * ppb + e][...], axis=0)
            km_ref[r] = s * (1.0 / (ppb * ps))


def _mlp(h, g, w_up, w_down, paged=None):
    m, d = h.shape
    dff = w_up.shape[1]
    tm = _tile(m, 512)
    tf = _tile(dff, 512)
    nf = dff // tf
    steps = (m // tm) * nf
    in_specs = [
        pl.BlockSpec((tm, d), lambda i, f, pt: (i, 0), pipeline_mode=pl.Buffered(1)),
        pl.BlockSpec((1, d), lambda i, f, pt: (0, 0)),
        pl.BlockSpec((d, tf), lambda i, f, pt: (0, f)),
        pl.BlockSpec((tf, d), lambda i, f, pt: (f, 0)),
    ]
    out_specs = [pl.BlockSpec((tm, d), lambda i, f, pt: (i, 0))]
    out_shape = [jax.ShapeDtypeStruct((m, d), F32)]
    operands = [h, g.reshape(1, d), w_up, w_down]
    n_page, ppb = 0, 1
    if paged is None:
        pt_flat = jnp.zeros((1,), jnp.int32)
    else:
        cache_k, layer, pt_flat, ppb = paged
        _, _, ps, n_heads, hd = cache_k.shape
        n_slots = pt_flat.shape[0]
        assert n_slots % (steps * ppb) == 0
        n_page = n_slots // steps
        for e in range(n_page):
            in_specs.append(pl.BlockSpec(
                (None, None, ps, n_heads, hd),
                functools.partial(lambda i, f, pt, e: (layer, pt[(i * nf + f) * n_page + e], 0, 0, 0), e=e)))
        operands += [cache_k] * n_page
        out_specs.append(pl.BlockSpec((n_page // ppb, n_heads, hd), lambda i, f, pt: (i * nf + f, 0, 0)))
        out_shape.append(jax.ShapeDtypeStruct((n_slots // ppb, n_heads, hd), F32))
    outs = pl.pallas_call(
        functools.partial(_mlp_kernel, n_page=n_page, ppb=ppb),
        grid_spec=pltpu.PrefetchScalarGridSpec(
            num_scalar_prefetch=1, grid=(m // tm, nf), in_specs=in_specs, out_specs=out_specs,
            scratch_shapes=[pltpu.VMEM((tm, d), BF16)]),
        out_shape=out_shape,
        compiler_params=_params("parallel", "arbitrary"),
        name="mlp",
    )(pt_flat, *operands)
    return outs[0] if paged is None else outs


def _ple_kernel(h_ref, p_ref, g_ref, gf_ref, wg_ref, wp_ref, y_ref, z_ref, pb_ref, *, final_norm):
    j = pl.program_id(1)
    tn = wg_ref.shape[1]

    rows = h_ref.shape[0]
    rc = _tile(rows, NORM_ROW_CHUNK)

    @pl.when(j == 0)
    def _():
        for r0 in range(0, rows, rc):
            z_ref[r0:r0 + rc, :] = _rms_scale(h_ref[r0:r0 + rc, :], g_ref[...]).astype(BF16)
        pb_ref[...] = p_ref[...].astype(BF16)

    cols = pl.ds(pl.multiple_of(j * tn, tn), tn)
    gate = jnp.dot(z_ref[...], wg_ref[...], preferred_element_type=F32)
    emb = jnp.dot(pb_ref[...], wp_ref[...].astype(BF16), preferred_element_type=F32)
    y_ref[:, cols] = h_ref[:, cols] + jax.nn.sigmoid(gate) * emb

    if final_norm:
        @pl.when(j == pl.num_programs(1) - 1)
        def _():
            for r0 in range(0, rows, rc):
                y_ref[r0:r0 + rc, :] = _rms_scale(y_ref[r0:r0 + rc, :], gf_ref[...])


def _ple(h, p, g_ple, g_final, w_gate, w_proj, layer, final_norm):
    m, d = h.shape
    pdim = p.shape[1]
    tm = _tile(m, 512)
    tn = _tile(d, 1024)
    vec = pl.BlockSpec((1, d), lambda i, j: (0, 0))
    return pl.pallas_call(
        functools.partial(_ple_kernel, final_norm=final_norm),
        grid=(m // tm, d // tn),
        in_specs=[
            pl.BlockSpec((tm, d), lambda i, j: (i, 0), pipeline_mode=pl.Buffered(1)),
            pl.BlockSpec((tm, pdim), lambda i, j: (i, 0)),
            vec, vec,
            pl.BlockSpec((d, tn), lambda i, j: (0, j)),
            pl.BlockSpec((None, pdim, tn), lambda i, j: (layer, 0, j)),
        ],
        out_specs=pl.BlockSpec((tm, d), lambda i, j: (i, 0)),
        out_shape=jax.ShapeDtypeStruct((m, d), F32),
        scratch_shapes=[pltpu.VMEM((tm, d), BF16), pltpu.VMEM((tm, pdim), BF16)],
        compiler_params=_params("parallel", "arbitrary"),
        name="ple",
    )(h, p, g_ple.reshape(1, d), g_final.reshape(1, d), w_gate, w_proj)


def _ln_swish(y, g, beta):
    mu = jnp.mean(y, axis=-1, keepdims=True)
    xc = y - mu
    var = jnp.mean(xc * xc, axis=-1, keepdims=True)
    yn = xc * lax.rsqrt(var + NORM_EPS) * g + beta
    return yn * jax.nn.sigmoid(yn)


def _conv_prompt_kernel(cur_ref, halo_ref, w_ref, b_ref, g_ref, beta_ref, o_ref, ext_ref, sh_ref, y_ref, *, kw, cc):
    t = pl.program_id(1)
    tt, c = cur_ref.shape
    ext_ref[0:CONV_HALO, :] = jnp.where(t == 0, 0.0, halo_ref[...])
    ext_ref[CONV_HALO:CONV_HALO + tt, :] = cur_ref[...]
    off = CONV_HALO - (kw - 1)
    rows = sh_ref.shape[1]
    for phase in range(1, SUBLANES):
        sh_ref[phase - 1] = ext_ref[phase:phase + rows, :]
    groups = tt // SUBLANES
    for c0 in range(0, c, cc):
        acc = None
        for k in range(kw):
            phase = (off + k) % SUBLANES
            a0 = off + k - phase
            src = ext_ref if phase == 0 else sh_ref.at[phase - 1]
            term = src[a0:a0 + tt, c0:c0 + cc].reshape(groups, SUBLANES, cc) * w_ref[k, :, c0:c0 + cc][None]
            acc = term if acc is None else acc + term
        y_ref[:, c0:c0 + cc] = acc.reshape(tt, cc) + b_ref[:, c0:c0 + cc]
    o_ref[...] = _ln_swish(y_ref[...], g_ref[...], beta_ref[...]).astype(o_ref.dtype)


def _conv_prompt(u, batch, seq, w_dw, b_dw, ln_g, ln_b):
    m, c = u.shape
    kw = w_dw.shape[0]
    assert kw - 1 <= CONV_HALO
    tt = _tile(seq, 128)
    assert tt % CONV_HALO == 0
    nt = seq // tt
    hb = tt // CONV_HALO
    vec = lambda a: a.reshape(1, c)
    vspec = pl.BlockSpec((1, c), lambda b, t: (0, 0))
    kern = functools.partial(_conv_prompt_kernel, kw=kw, cc=min(LANES, c))
    w_rep = jnp.broadcast_to(w_dw[:, None, :], (kw, SUBLANES, c))
    return pl.pallas_call(
        kern,
        grid=(batch, nt),
        in_specs=[
            pl.BlockSpec((tt, c), lambda b, t: (b * nt + t, 0)),
            pl.BlockSpec((CONV_HALO, c), lambda b, t: (jnp.maximum((b * nt + t) * hb - 1, 0), 0)),
            pl.BlockSpec((kw, SUBLANES, c), lambda b, t: (0, 0, 0)),
            vspec, vspec, vspec,
        ],
        out_specs=pl.BlockSpec((tt, c), lambda b, t: (b * nt + t, 0)),
        out_shape=jax.ShapeDtypeStruct((m, c), BF16),
        scratch_shapes=[pltpu.VMEM((CONV_HALO + tt, c), F32),
                        pltpu.VMEM((SUBLANES - 1, CONV_HALO + tt - SUBLANES, c), F32),
                        pltpu.VMEM((tt, c), F32)],
        compiler_params=_params("parallel", "parallel"),
        name="conv_prompt",
    )(u, u, w_rep, vec(b_dw), vec(ln_g), vec(ln_b))


def _conv_sample_kernel(st_ref, u_ref, w_ref, b_ref, g_ref, beta_ref, o_ref, ns_ref):
    ks = st_ref.shape[0]
    st = st_ref[...]
    u = u_ref[...]
    y = jnp.sum(st * w_ref[0:ks, :], axis=0, keepdims=True) + u * w_ref[ks:ks + 1, :] + b_ref[...]
    o_ref[...] = _ln_swish(y, g_ref[...], beta_ref[...]).astype(o_ref.dtype)
    ns_ref[0:ks - 1, :] = st_ref[1:ks, :]
    ns_ref[ks - 1:ks, :] = u


def _conv_sample(u, state, w_dw, b_dw, ln_g, ln_b):
    db, c = u.shape
    ks = state.shape[1]
    vec = lambda a: a.reshape(1, c)
    vspec = pl.BlockSpec((1, c), lambda b: (0, 0))
    out, new_state = pl.pallas_call(
        _conv_sample_kernel,
        grid=(db,),
        in_specs=[
            pl.BlockSpec((None, ks, c), lambda b: (b, 0, 0)),
            pl.BlockSpec((None, 1, c), lambda b: (b, 0, 0)),
            pl.BlockSpec((ks + 1, c), lambda b: (0, 0)),
            vspec, vspec, vspec,
        ],
        out_specs=[pl.BlockSpec((None, 1, c), lambda b: (b, 0, 0)), pl.BlockSpec((None, ks, c), lambda b: (b, 0, 0))],
        out_shape=[jax.ShapeDtypeStruct((db, 1, c), F32), jax.ShapeDtypeStruct((db, ks, c), F32)],
        compiler_params=_params("parallel"),
        name="conv_sample",
    )(state, u.reshape(db, 1, c), w_dw, vec(b_dw), vec(ln_g), vec(ln_b))
    return out.reshape(db, c), new_state


def _attn_block_body(jj, q_ref, kb_ref, vt_ref, km_ref, o_ref, scale):
    blk = MOBA_BLOCK
    q = q_ref[jj * blk:(jj + 1) * blk, :]
    qb = q.astype(BF16)
    qs = (q * (scale * LOG2_E)).astype(BF16)
    ln = (jj + 1) * blk
    nt = (((1,), (1,)), ((), ()))
    st = lax.dot_general(kb_ref[0:ln, :], qs, nt, preferred_element_type=F32)
    pieces = []
    if jj > MOBA_TOPK:
        g = lax.dot_general(km_ref[...].astype(BF16), qb, nt, preferred_element_type=F32)
        for n in range(jj):
            cnt = jnp.zeros((1, blk), F32)
            for m in range(jj):
                if m == n:
                    continue
                beats = (g[m:m + 1] >= g[n:n + 1]) if m < n else (g[m:m + 1] > g[n:n + 1])
                cnt = cnt + jnp.where(beats, 1.0, 0.0)
            pieces.append(jnp.where(cnt < MOBA_TOPK, st[n * blk:(n + 1) * blk, :], -jnp.inf))
    else:
        pieces = [st[n * blk:(n + 1) * blk, :] for n in range(jj)]
    key = lax.broadcasted_iota(jnp.int32, (blk, blk), 0)
    qry = lax.broadcasted_iota(jnp.int32, (blk, blk), 1)
    pieces.append(jnp.where(key <= qry, st[jj * blk:ln, :], -jnp.inf))
    mx = jnp.max(pieces[-1], axis=0, keepdims=True)
    for pc in pieces[:-1]:
        mx = jnp.maximum(mx, jnp.max(pc, axis=0, keepdims=True))
    ps = [jnp.exp2(pc - mx) for pc in pieces]
    den = ps[0].sum(axis=0, keepdims=True)
    for p in ps[1:]:
        den = den + p.sum(axis=0, keepdims=True)
    pcat = jnp.concatenate(ps, axis=0) if len(ps) > 1 else ps[0]
    acc_t = jnp.dot(vt_ref[:, 0:ln], pcat.astype(BF16), preferred_element_type=F32)
    o_ref[jj * blk:(jj + 1) * blk, :] = (acc_t / den).T.astype(o_ref.dtype)


def _attn_prompt_kernel(q_ref, k_ref, v_ref, o_ref, kb_ref, vt_ref, km_ref, *, nb, scale):
    blk = MOBA_BLOCK
    kb_ref[...] = k_ref[...].astype(BF16)
    km_ref[...] = jnp.zeros(km_ref.shape, F32)
    for n in range(nb):
        vt_ref[:, n * blk:(n + 1) * blk] = v_ref[n * blk:(n + 1) * blk, :].T.astype(BF16)
        km_ref[n:n + 1, :] = jnp.mean(k_ref[n * blk:(n + 1) * blk, :], axis=0, keepdims=True)
    for jj in range(nb):
        _attn_block_body(jj, q_ref, kb_ref, vt_ref, km_ref, o_ref, scale)


def _attn_prompt(q, k, v, batch, seq, n_heads, hd):
    assert seq % MOBA_BLOCK == 0
    nb = seq // MOBA_BLOCK
    m = batch * seq
    bf16_rows = 2 * SUBLANES
    km_rows = -(-nb // bf16_rows) * bf16_rows
    kern = functools.partial(_attn_prompt_kernel, nb=nb, scale=hd ** -0.5)
    spec = pl.BlockSpec((seq, hd), lambda b, h: (b, h))
    return pl.pallas_call(
        kern,
        grid=(batch, n_heads),
        in_specs=[spec, spec, spec],
        out_specs=spec,
        out_shape=jax.ShapeDtypeStruct((m, n_heads * hd), BF16),
        scratch_shapes=[pltpu.VMEM((seq, hd), BF16), pltpu.VMEM((hd, seq), BF16), pltpu.VMEM((km_rows, hd), F32)],
        compiler_params=_params("parallel", "parallel"),
        name="attn_prompt",
    )(q, k, v)


def _topk_kernel(km_ref, q_ref, o_ref):
    nblk = km_ref.shape[0]
    gate = jnp.sum(km_ref[...] * q_ref[...][None], axis=-1, keepdims=True)
    blk_id = lax.broadcasted_iota(jnp.int32, gate.shape, 0)
    for s in range(MOBA_TOPK):
        mx = jnp.max(gate, axis=0, keepdims=True)
        idx = jnp.min(jnp.where(gate == mx, blk_id, nblk), axis=0, keepdims=True)
        o_ref[s:s + 1] = idx
        gate = jnp.where(blk_id == idx, -jnp.inf, gate)


def _sample_topk(kmean, q):
    db, nblk, n_heads, hd = kmean.shape
    assert nblk >= MOBA_TOPK
    out = pl.pallas_call(
        _topk_kernel,
        grid=(db,),
        in_specs=[pl.BlockSpec((None, nblk, n_heads, hd), lambda b: (b, 0, 0, 0)),
                  pl.BlockSpec((None, n_heads, hd), lambda b: (b, 0, 0))],
        out_specs=pl.BlockSpec((None, MOBA_TOPK, n_heads, 1), lambda b: (b, 0, 0, 0)),
        out_shape=jax.ShapeDtypeStruct((db, MOBA_TOPK, n_heads, 1), jnp.int32),
        compiler_params=_params("parallel"),
        name="sample_topk",
    )(kmean, q)
    return jnp.swapaxes(out.reshape(db, MOBA_TOPK, n_heads), 1, 2)


def _attn_sample_kernel(pt_ref, sel_ref, q_ref, kn_ref, vn_ref, ck_hbm, cv_hbm, o_ref, kbuf, vbuf, sems, *,
                        layer, n_pages, ppb, scale):
    b = pl.program_id(0)
    n_heads, hd = q_ref.shape
    n_slab = MOBA_TOPK * ppb

    def slab_copies(h, slot):
        cps = []
        for s in range(MOBA_TOPK):
            blk = sel_ref[(b * n_heads + h) * MOBA_TOPK + s]
            for e in range(ppb):
                page = pt_ref[b * n_pages + blk * ppb + e]
                i = s * ppb + e
                cps.append(pltpu.make_async_copy(ck_hbm.at[layer, page, :, h, :], kbuf.at[slot, i], sems.at[0, slot, i]))
                cps.append(pltpu.make_async_copy(cv_hbm.at[layer, page, :, h, :], vbuf.at[slot, i], sems.at[1, slot, i]))
        return cps

    n_slot = kbuf.shape[0]
    group = SAMPLE_HEAD_GROUP
    assert n_heads % group == 0 and n_slot % group == 0 and n_slot > group
    n_key = n_slab * kbuf.shape[2]

    def group_copies(g):
        return [cp for h in range(g * group, (g + 1) * group) for cp in slab_copies(h, h % n_slot)]

    n_group = n_heads // group
    ahead = n_slot // group - 1
    for g0 in range(min(ahead, n_group)):
        for cp in group_copies(g0):
            cp.start()
    for h in range(n_heads):
        slot = h % n_slot
        if h % group == 0:
            g = h // group
            if g + ahead < n_group:
                for cp in group_copies(g + ahead):
                    cp.start()
            for cp in group_copies(g):
                cp.wait()
        q = q_ref[h:h + 1, :]
        qm = jnp.broadcast_to(q, (LANES, hd)).astype(BF16)
        keys = kbuf[slot].reshape(n_key, hd).astype(BF16)
        s = lax.dot_general(keys, qm, (((1,), (1,)), ((), ())), preferred_element_type=F32) * scale
        s_self = jnp.sum(q * kn_ref[h:h + 1, :], axis=1, keepdims=True) * scale
        mx = jnp.maximum(jnp.max(s, axis=0, keepdims=True), s_self)
        p = jnp.exp(s - mx)
        p_self = jnp.exp(s_self - mx)
        den = jnp.sum(p, axis=0, keepdims=True) + p_self
        acc = jnp.sum(p * vbuf[slot].reshape(n_key, hd), axis=0, keepdims=True) + p_self * vn_ref[h:h + 1, :]
        o_ref[h:h + 1, :] = acc / den


def _attn_sample(q, k_new, v_new, cache_k, cache_v, layer, pt_flat, sel_flat, n_pages, ppb):
    db, n_heads, hd = q.shape
    ps = cache_k.shape[2]
    n_slab = MOBA_TOPK * ppb
    n_slot = SAMPLE_HEAD_SLOTS
    assert hd == LANES
    row_spec = pl.BlockSpec((None, n_heads, hd), lambda b, pt, sel: (b, 0, 0))
    any_spec = pl.BlockSpec(memory_space=pl.ANY)
    out = pl.pallas_call(
        functools.partial(_attn_sample_kernel, layer=layer, n_pages=n_pages, ppb=ppb, scale=hd ** -0.5),
        grid_spec=pltpu.PrefetchScalarGridSpec(
            num_scalar_prefetch=2,
            grid=(db,),
            in_specs=[row_spec, row_spec, row_spec, any_spec, any_spec],
            out_specs=row_spec,
            scratch_shapes=[pltpu.VMEM((n_slot, n_slab, ps, hd), F32), pltpu.VMEM((n_slot, n_slab, ps, hd), F32),
                            pltpu.SemaphoreType.DMA((2, n_slot, n_slab))],
        ),
        out_shape=jax.ShapeDtypeStruct((db, n_heads, hd), F32),
        compiler_params=_params("arbitrary"),
        name="attn_sample",
    )(pt_flat, sel_flat, q, k_new, v_new, cache_k, cache_v)
    return out.reshape(db, n_heads * hd)


def _in_projection(z, z_s, w_in, layer, c, a, rope_p, rope_s, table_rows, hd):
    m, ms = z.shape[0], z_s.shape[0]
    tm = _tile(m, 1024)
    tn_glu = _tile(c, 256)
    tn = _tile(a, 512)
    assert table_rows % tm == 0
    period = table_rows // tm
    tbl = [(t, (tm, hd), lambda j, i: (i % period, 0)) for t in rope_p]
    tbl_s = [(t, (ms, hd), lambda j, i: (0, 0)) for t in rope_s]
    col = lambda off, group=0: (0, w_in, layer, 0, off, group)
    u = _matmul([z], [col(0), col(c, 1)], [], _ep_glu, c, tn_glu, F32, "in_proj_glu", side=([z_s], []))
    q = _matmul([z], [col(2 * c)], tbl, _ep_rope, a, tn, F32, "in_proj_q", side=([z_s], tbl_s))
    k = _matmul([z], [col(2 * c + a)], tbl, _ep_rope, a, tn, F32, "in_proj_k", side=([z_s], tbl_s))
    v = _matmul([z], [col(2 * c + 2 * a)], [], _ep_plain, a, tn, F32, "in_proj_v", side=([z_s], []))
    return tuple(zip(u, q, k, v))


def _post_mixer(x, conv_out, attn, p, layer, w_out, g_ffn, w_up, w_down, g_ple, w_ple_gate, w_ple_proj, g_final,
                final_norm, paged=None):
    m, d = x.shape
    c = conv_out.shape[1]
    assert attn.shape[1] == c
    tm = _tile(m, 1024)
    tn = _tile(d, 512)
    resid = lambda r: (r, (tm, tn), lambda j, i: (i, j))
    h = _matmul([conv_out, attn], [(0, w_out, layer, 0, 0, 0), (1, w_out, layer, 1, 0, 0)], [resid(x)], _ep_resid,
                d, tn, F32, "out_proj")
    h = _mlp(h, g_ffn, w_up, w_down, paged)
    if paged is not None:
        h, kmean = h
    out = _ple(h, p, g_ple, g_final, w_ple_gate, w_ple_proj, layer, final_norm)
    return out if paged is None else (out, kmean)


def kernel(x_prompt, x_sample, cache_k, cache_v, state_conv, page_table, p_prompt, p_sample, g_mix, w_in, w_dw, b_dw, g_conv_ln, b_conv_ln, w_out, g_ffn, w_up, w_down, g_ple, w_ple_gate, w_ple_proj, g_final):
    batch, seq, d = x_prompt.shape
    db, dseq, _ = x_sample.shape
    assert dseq == 1
    depth, n_pool, page, n_heads, hd = cache_k.shape
    n_pages = page_table.shape[1]
    past_len = n_pages * page
    assert MOBA_BLOCK % page == 0 and past_len % MOBA_BLOCK == 0
    ppb = MOBA_BLOCK // page
    c = w_dw.shape[-1]
    a = n_heads * hd
    assert c + a == d

    hp = x_prompt.reshape(batch * seq, d)
    hs = x_sample.reshape(db, d)
    cos_p, sin_p = _rope_tables(0, seq, hd)
    cos_s, sin_s = _rope_tables(past_len, 1, hd)
    cos_s = jnp.broadcast_to(cos_s, (db, hd))
    sin_s = jnp.broadcast_to(sin_s, (db, hd))
    pt_flat = page_table.reshape(-1).astype(jnp.int32)

    kp_l, vp_l, cp_l, ks_l, vs_l, cs_l = [], [], [], [], [], []
    for l in range(depth):
        post = (l, w_out, g_ffn[l], w_up[l].astype(BF16), w_down[l].astype(BF16), g_ple[l],
                w_ple_gate[l].astype(BF16), w_ple_proj, g_final, l == depth - 1)
        conv_w = (w_dw[l], b_dw[l], g_conv_ln[l], b_conv_ln[l])

        z = _rmsnorm(hp, g_mix[l], BF16)
        z_s = _rmsnorm(hs, g_mix[l], BF16)
        (u, q, k, v), (u_s, q_s, k_s, v_s) = _in_projection(z, z_s, w_in, l, c, a, (cos_p, sin_p), (cos_s, sin_s),
                                                            seq, hd)

        conv_out = _conv_prompt(u, batch, seq, *conv_w)
        attn = _attn_prompt(q, k, v, batch, seq, n_heads, hd)
        hp, kmean = _post_mixer(hp, conv_out, attn, p_prompt[l].reshape(batch * seq, -1), *post,
                                paged=(cache_k, l, pt_flat, ppb))
        kp_l.append(k.reshape(batch, seq, n_heads, hd))
        vp_l.append(v.reshape(batch, seq, n_heads, hd))
        cp_l.append(u.reshape(batch, seq, c)[:, seq - (w_dw.shape[1] - 1):, :])

        conv_out, new_state = _conv_sample(u_s, state_conv[l], *conv_w)
        h3 = lambda t: t.reshape(db, n_heads, hd)
        sel = _sample_topk(kmean.reshape(db, n_pages // ppb, n_heads, hd), h3(q_s))
        attn = _attn_sample(h3(q_s), h3(k_s), h3(v_s), cache_k, cache_v, l, pt_flat, sel.reshape(-1), n_pages, ppb)
        hs = _post_mixer(hs, conv_out.astype(BF16), attn.astype(BF16), p_sample[l].reshape(db, -1), *post)
        ks_l.append(k_s.reshape(db, 1, n_heads, hd))
        vs_l.append(v_s.reshape(db, 1, n_heads, hd))
        cs_l.append(new_state)

    y_prompt = hp.reshape(batch, seq, d)
    y_sample = hs.reshape(db, 1, d)
    stack = lambda xs: xs[0][None] if len(xs) == 1 else jnp.stack(xs)
    return (y_prompt, y_sample, stack(kp_l), stack(vp_l), stack(cp_l), stack(ks_l), stack(vs_l), stack(cs_l))
```

```python
import functools

import jax
import jax.numpy as jnp
from jax import lax
from jax.experimental import pallas as pl
from jax.experimental.pallas import tpu as pltpu

F32 = jnp.float32
BF16 = jnp.bfloat16

MOBA_BLOCK = 256
MOBA_TOPK = 3
ROPE_THETA = 10000.0
NORM_EPS = 1e-6
LOG2_E = 1.4426950408889634
LANES = 128
SUBLANES = 8
CONV_HALO = 32
MM_ROW_CHUNK = 512
NORM_ROW_CHUNK = 64
SAMPLE_HEAD_GROUP = 4
SAMPLE_GROUPS_BUFFERED = 3
VMEM_LIMIT = 60 * 1024 * 1024


def _params(*sem):
    return pltpu.CompilerParams(dimension_semantics=sem, vmem_limit_bytes=VMEM_LIMIT)


def _tile(n, pref):
    if n <= pref:
        return n
    t = pref
    while n % t:
        t //= 2
    return t


def _rmsnorm_kernel(x_ref, g_ref, o_ref):
    x = x_ref[...]
    ms = jnp.mean(x * x, axis=-1, keepdims=True)
    o_ref[...] = (x * lax.rsqrt(ms + NORM_EPS) * g_ref[...]).astype(o_ref.dtype)


def _rmsnorm(x, g, out_dtype):
    m, d = x.shape
    tm = _tile(m, 256)
    return pl.pallas_call(
        _rmsnorm_kernel,
        grid=(m // tm,),
        in_specs=[pl.BlockSpec((tm, d), lambda i: (i, 0)), pl.BlockSpec((1, d), lambda i: (0, 0))],
        out_specs=pl.BlockSpec((tm, d), lambda i: (i, 0)),
        out_shape=jax.ShapeDtypeStruct((m, d), out_dtype),
        compiler_params=_params("parallel"),
        name="rmsnorm",
    )(x, g.reshape(1, d))


def _mm_kernel(*refs, n_lhs, dots, n_extra, epilogue, side):
    n_dot = len(dots)
    groups = 2 if side else 1
    pos = 0
    lhs = [refs[pos + g * n_lhs:pos + (g + 1) * n_lhs] for g in range(groups)]
    pos += groups * n_lhs
    ws = refs[pos:pos + n_dot]
    pos += n_dot
    extras = [refs[pos + g * n_extra:pos + (g + 1) * n_extra] for g in range(groups)]
    pos += groups * n_extra
    outs = refs[pos:pos + groups]
    wbs = refs[pos + groups:pos + groups + n_dot]
    i = pl.program_id(1)

    @pl.when(i == 0)
    def _():
        for w, wb in zip(ws, wbs):
            wb[...] = w[...].astype(BF16)

    def run(g):
        rows = outs[g].shape[0]
        rc = MM_ROW_CHUNK if rows % MM_ROW_CHUNK == 0 else rows
        for r0 in range(0, rows, rc):
            accs = {}
            for (li, group), wb in zip(dots, wbs):
                a = lhs[g][li][r0:r0 + rc, :]
                if a.dtype != BF16:
                    a = a.astype(BF16)
                d = jnp.dot(a, wb[...], preferred_element_type=F32)
                accs[group] = d if group not in accs else accs[group] + d
            res = epilogue([accs[k] for k in sorted(accs)], [e[r0:r0 + rc, :] for e in extras[g]])
            outs[g][r0:r0 + rc, :] = res.astype(outs[g].dtype)

    run(0)
    if side:
        pl.when(i == pl.num_programs(1) - 1)(functools.partial(run, 1))


def _matmul(lhs, dots, extras, epilogue, n_cols, tn, out_dtype, name, side=None, tm_pref=1024):
    m = lhs[0].shape[0]
    tm = _tile(m, tm_pref)
    in_specs = [pl.BlockSpec((tm, a.shape[1]), lambda j, i: (i, 0)) for a in lhs]
    operands = list(lhs)
    if side:
        lhs_s, extras_s = side
        ms = lhs_s[0].shape[0]
        in_specs += [pl.BlockSpec((ms, a.shape[1]), lambda j, i: (0, 0)) for a in lhs_s]
        operands += list(lhs_s)
    scratch = []
    for li, w, layer, rb, co, _ in dots:
        kl = lhs[li].shape[1]
        assert co % tn == 0
        in_specs.append(pl.BlockSpec(
            (None, kl, tn), functools.partial(lambda j, i, layer, rb, cb: (layer, rb, cb + j), layer=layer, rb=rb,
                                              cb=co // tn)))
        operands.append(w)
        scratch.append(pltpu.VMEM((kl, tn), BF16))
    for arr, blk, imap in list(extras) + (list(extras_s) if side else []):
        in_specs.append(pl.BlockSpec(blk, imap))
        operands.append(arr)
    out_specs = [pl.BlockSpec((tm, tn), lambda j, i: (i, j))]
    out_shape = [jax.ShapeDtypeStruct((m, n_cols), out_dtype)]
    if side:
        out_specs.append(pl.BlockSpec((ms, tn), lambda j, i: (0, j)))
        out_shape.append(jax.ShapeDtypeStruct((ms, n_cols), out_dtype))
    kern = functools.partial(_mm_kernel, n_lhs=len(lhs), dots=[(d[0], d[5]) for d in dots],
                             n_extra=len(extras), epilogue=epilogue, side=bool(side))
    outs = pl.pallas_call(
        kern,
        grid=(n_cols // tn, m // tm),
        in_specs=in_specs,
        out_specs=out_specs,
        out_shape=out_shape,
        scratch_shapes=scratch,
        compiler_params=_params("parallel", "arbitrary"),
        name=name,
    )(*operands)
    return outs if side else outs[0]


def _ep_plain(accs, extras):
    return accs[0]


def _ep_glu(accs, extras):
    return accs[0] * jax.nn.sigmoid(accs[1])


def _ep_resid(accs, extras):
    return extras[0] + accs[0]


def _ep_rope(accs, extras):
    cos, sin_signed = extras
    y = accs[0]
    hd = cos.shape[1]
    outs = []
    for h in range(y.shape[1] // hd):
        xh = y[:, h * hd:(h + 1) * hd]
        outs.append(xh * cos + pltpu.roll(xh, hd // 2, 1) * sin_signed)
    return jnp.concatenate(outs, axis=1)


def _rope_tables(pos0, s, hd):
    inv = 1.0 / (ROPE_THETA ** (jnp.arange(0, hd, 2, dtype=F32) / hd))
    pos = (jnp.arange(s, dtype=jnp.int32) + pos0).astype(F32)
    ang = pos[:, None] * inv[None, :]
    cos, sin = jnp.cos(ang), jnp.sin(ang)
    return jnp.concatenate([cos, cos], axis=-1), jnp.concatenate([-sin, sin], axis=-1)


def _rms_scale(x, g):
    ms = jnp.mean(x * x, axis=-1, keepdims=True)
    return x * lax.rsqrt(ms + NORM_EPS) * g


def _mlp_kernel(pt_ref, h_ref, g_ref, wu_ref, wd_ref, *refs, n_page, ppb):
    page_refs = refs[:n_page]
    o_ref = refs[n_page]
    z_ref = refs[-1]
    f = pl.program_id(1)

    @pl.when(f == 0)
    def _():
        rows = h_ref.shape[0]
        rc = _tile(rows, NORM_ROW_CHUNK)
        for r0 in range(0, rows, rc):
            h = h_ref[r0:r0 + rc, :]
            o_ref[r0:r0 + rc, :] = h
            z_ref[r0:r0 + rc, :] = _rms_scale(h, g_ref[...]).astype(BF16)

    hm = jnp.dot(z_ref[...], wu_ref[...], preferred_element_type=F32)
    hm = jnp.square(jnp.maximum(hm, 0.0)).astype(BF16)
    o_ref[...] += jnp.dot(hm, wd_ref[...], preferred_element_type=F32)

    if n_page:
        km_ref = refs[n_page + 1]
        ps = page_refs[0].shape[0]
        for r in range(n_page // ppb):
            s = jnp.sum(page_refs[r * ppb][...], axis=0)
            for e in range(1, ppb):
                s = s + jnp.sum(page_refs[r * ppb + e][...], axis=0)
            km_ref[r] = s * (1.0 / (ppb * ps))


def _mlp(h, g, w_up, w_down, paged=None):
    m, d = h.shape
    dff = w_up.shape[1]
    tm = _tile(m, 512)
    tf = _tile(dff, 512)
    nf = dff // tf
    steps = (m // tm) * nf
    in_specs = [
        pl.BlockSpec((tm, d), lambda i, f, pt: (i, 0), pipeline_mode=pl.Buffered(1)),
        pl.BlockSpec((1, d), lambda i, f, pt: (0, 0)),
        pl.BlockSpec((d, tf), lambda i, f, pt: (0, f)),
        pl.BlockSpec((tf, d), lambda i, f, pt: (f, 0)),
    ]
    out_specs = [pl.BlockSpec((tm, d), lambda i, f, pt: (i, 0))]
    out_shape = [jax.ShapeDtypeStruct((m, d), F32)]
    operands = [h, g.reshape(1, d), w_up, w_down]
    n_page, ppb = 0, 1
    if paged is None:
        pt_flat = jnp.zeros((1,), jnp.int32)
    else:
        cache_k, layer, pt_flat, ppb = paged
        _, _, ps, n_heads, hd = cache_k.shape
        n_slots = pt_flat.shape[0]
        assert n_slots % (steps * ppb) == 0
        n_page = n_slots // steps
        for e in range(n_page):
            in_specs.append(pl.BlockSpec(
                (None, None, ps, n_heads, hd),
                functools.partial(lambda i, f, pt, e: (layer, pt[(i * nf + f) * n_page + e], 0, 0, 0), e=e)))
        operands += [cache_k] * n_page
        out_specs.append(pl.BlockSpec((n_page // ppb, n_heads, hd), lambda i, f, pt: (i * nf + f, 0, 0)))
        out_shape.append(jax.ShapeDtypeStruct((n_slots // ppb, n_heads, hd), F32))
    outs = pl.pallas_call(
        functools.partial(_mlp_kernel, n_page=n_page, ppb=ppb),
        grid_spec=pltpu.PrefetchScalarGridSpec(
            num_scalar_prefetch=1, grid=(m // tm, nf), in_specs=in_specs, out_specs=out_specs,
            scratch_shapes=[pltpu.VMEM((tm, d), BF16)]),
        out_shape=out_shape,
        compiler_params=_params("parallel", "arbitrary"),
        name="mlp",
    )(pt_flat, *operands)
    return outs[0] if paged is None else outs


def _ple_kernel(h_ref, p_ref, g_ref, gf_ref, wg_ref, wp_ref, y_ref, z_ref, pb_ref, *, final_norm):
    j = pl.program_id(1)
    tn = wg_ref.shape[1]

    rows = h_ref.shape[0]
    rc = _tile(rows, NORM_ROW_CHUNK)

    @pl.when(j == 0)
    def _():
        for r0 in range(0, rows, rc):
            z_ref[r0:r0 + rc, :] = _rms_scale(h_ref[r0:r0 + rc, :], g_ref[...]).astype(BF16)
        pb_ref[...] = p_ref[...].astype(BF16)

    cols = pl.ds(pl.multiple_of(j * tn, tn), tn)
    gate = jnp.dot(z_ref[...], wg_ref[...], preferred_element_type=F32)
    emb = jnp.dot(pb_ref[...], wp_ref[...].astype(BF16), preferred_element_type=F32)
    y_ref[:, cols] = h_ref[:, cols] + jax.nn.sigmoid(gate) * emb

    if final_norm:
        @pl.when(j == pl.num_programs(1) - 1)
        def _():
            for r0 in range(0, rows, rc):
                y_ref[r0:r0 + rc, :] = _rms_scale(y_ref[r0:r0 + rc, :], gf_ref[...])


def _ple(h, p, g_ple, g_final, w_gate, w_proj, layer, final_norm):
    m, d = h.shape
    pdim = p.shape[1]
    tm = _tile(m, 512)
    tn = _tile(d, 1024)
    vec = pl.BlockSpec((1, d), lambda i, j: (0, 0))
    return pl.pallas_call(
        functools.partial(_ple_kernel, final_norm=final_norm),
        grid=(m // tm, d // tn),
        in_specs=[
            pl.BlockSpec((tm, d), lambda i, j: (i, 0), pipeline_mode=pl.Buffered(1)),
            pl.BlockSpec((tm, pdim), lambda i, j: (i, 0)),
            vec, vec,
            pl.BlockSpec((d, tn), lambda i, j: (0, j)),
            pl.BlockSpec((None, pdim, tn), lambda i, j: (layer, 0, j)),
        ],
        out_specs=pl.BlockSpec((tm, d), lambda i, j: (i, 0)),
        out_shape=jax.ShapeDtypeStruct((m, d), F32),
        scratch_shapes=[pltpu.VMEM((tm, d), BF16), pltpu.VMEM((tm, pdim), BF16)],
        compiler_params=_params("parallel", "arbitrary"),
        name="ple",
    )(h, p, g_ple.reshape(1, d), g_final.reshape(1, d), w_gate, w_proj)


def _cast_stream(weights, layer, steps, step_of):
    in_specs, out_specs, out_shapes = [], [], []
    for w in weights:
        _, r, c = w.shape
        rows = r // steps
        assert r % steps == 0 and rows % (2 * SUBLANES) == 0
        in_specs.append(pl.BlockSpec((None, rows, c), lambda *g: (layer, step_of(*g), 0)))
        out_specs.append(pl.BlockSpec((rows, c), lambda *g: (step_of(*g), 0)))
        out_shapes.append(jax.ShapeDtypeStruct((r, c), BF16))
    return in_specs, out_specs, out_shapes, list(weights)


def _cast_rows(src_refs, dst_refs):
    for src, dst in zip(src_refs, dst_refs):
        dst[...] = src[...].astype(BF16)


def _ln_swish(y, g, beta):
    mu = jnp.mean(y, axis=-1, keepdims=True)
    xc = y - mu
    var = jnp.mean(xc * xc, axis=-1, keepdims=True)
    yn = xc * lax.rsqrt(var + NORM_EPS) * g + beta
    return yn * jax.nn.sigmoid(yn)


def _conv_prompt_kernel(cur_ref, halo_ref, w_ref, b_ref, g_ref, beta_ref, *refs, kw, cc, n_cast):
    cast_src, refs = refs[:n_cast], refs[n_cast:]
    o_ref, cast_dst = refs[0], refs[1:1 + n_cast]
    ext_ref, sh_ref, y_ref = refs[1 + n_cast:]
    _cast_rows(cast_src, cast_dst)
    t = pl.program_id(1)
    tt, c = cur_ref.shape
    ext_ref[0:CONV_HALO, :] = jnp.where(t == 0, 0.0, halo_ref[...])
    ext_ref[CONV_HALO:CONV_HALO + tt, :] = cur_ref[...]
    off = CONV_HALO - (kw - 1)
    rows = sh_ref.shape[1]
    for phase in range(1, SUBLANES):
        sh_ref[phase - 1] = ext_ref[phase:phase + rows, :]
    groups = tt // SUBLANES
    for c0 in range(0, c, cc):
        acc = None
        for k in range(kw):
            phase = (off + k) % SUBLANES
            a0 = off + k - phase
            src = ext_ref if phase == 0 else sh_ref.at[phase - 1]
            term = src[a0:a0 + tt, c0:c0 + cc].reshape(groups, SUBLANES, cc) * w_ref[k, :, c0:c0 + cc][None]
            acc = term if acc is None else acc + term
        y_ref[:, c0:c0 + cc] = acc.reshape(tt, cc) + b_ref[:, c0:c0 + cc]
    o_ref[...] = _ln_swish(y_ref[...], g_ref[...], beta_ref[...]).astype(o_ref.dtype)


def _conv_prompt(u, batch, seq, w_dw, b_dw, ln_g, ln_b, cast=(), layer=0):
    m, c = u.shape
    kw = w_dw.shape[0]
    assert kw - 1 <= CONV_HALO
    tt = _tile(seq, 128)
    assert tt % CONV_HALO == 0
    nt = seq // tt
    hb = tt // CONV_HALO
    vec = lambda a: a.reshape(1, c)
    vspec = pl.BlockSpec((1, c), lambda b, t: (0, 0))
    kern = functools.partial(_conv_prompt_kernel, kw=kw, cc=min(LANES, c), n_cast=len(cast))
    w_rep = jnp.broadcast_to(w_dw[:, None, :], (kw, SUBLANES, c))
    c_in, c_out, c_shape, c_ops = _cast_stream(cast, layer, batch * nt, lambda b, t: b * nt + t)
    outs = pl.pallas_call(
        kern,
        grid=(batch, nt),
        in_specs=[
            pl.BlockSpec((tt, c), lambda b, t: (b * nt + t, 0)),
            pl.BlockSpec((CONV_HALO, c), lambda b, t: (jnp.maximum((b * nt + t) * hb - 1, 0), 0)),
            pl.BlockSpec((kw, SUBLANES, c), lambda b, t: (0, 0, 0)),
            vspec, vspec, vspec,
        ] + c_in,
        out_specs=[pl.BlockSpec((tt, c), lambda b, t: (b * nt + t, 0))] + c_out,
        out_shape=[jax.ShapeDtypeStruct((m, c), BF16)] + c_shape,
        scratch_shapes=[pltpu.VMEM((CONV_HALO + tt, c), F32),
                        pltpu.VMEM((SUBLANES - 1, CONV_HALO + tt - SUBLANES, c), F32),
                        pltpu.VMEM((tt, c), F32)],
        compiler_params=_params("parallel", "parallel"),
        name="conv_prompt",
    )(u, u, w_rep, vec(b_dw), vec(ln_g), vec(ln_b), *c_ops)
    return outs[0] if not cast else outs


def _conv_sample_kernel(st_ref, u_ref, w_ref, b_ref, g_ref, beta_ref, o_ref, ns_ref):
    ks = st_ref.shape[0]
    st = st_ref[...]
    u = u_ref[...]
    y = jnp.sum(st * w_ref[0:ks, :], axis=0, keepdims=True) + u * w_ref[ks:ks + 1, :] + b_ref[...]
    o_ref[...] = _ln_swish(y, g_ref[...], beta_ref[...]).astype(o_ref.dtype)
    ns_ref[0:ks - 1, :] = st_ref[1:ks, :]
    ns_ref[ks - 1:ks, :] = u


def _conv_sample(u, state, w_dw, b_dw, ln_g, ln_b):
    db, c = u.shape
    ks = state.shape[1]
    vec = lambda a: a.reshape(1, c)
    vspec = pl.BlockSpec((1, c), lambda b: (0, 0))
    out, new_state = pl.pallas_call(
        _conv_sample_kernel,
        grid=(db,),
        in_specs=[
            pl.BlockSpec((None, ks, c), lambda b: (b, 0, 0)),
            pl.BlockSpec((None, 1, c), lambda b: (b, 0, 0)),
            pl.BlockSpec((ks + 1, c), lambda b: (0, 0)),
            vspec, vspec, vspec,
        ],
        out_specs=[pl.BlockSpec((None, 1, c), lambda b: (b, 0, 0)), pl.BlockSpec((None, ks, c), lambda b: (b, 0, 0))],
        out_shape=[jax.ShapeDtypeStruct((db, 1, c), F32), jax.ShapeDtypeStruct((db, ks, c), F32)],
        compiler_params=_params("parallel"),
        name="conv_sample",
    )(state, u.reshape(db, 1, c), w_dw, vec(b_dw), vec(ln_g), vec(ln_b))
    return out.reshape(db, c), new_state


def _attn_block_body(jj, q_ref, kb_ref, vt_ref, km_ref, o_ref, scale):
    blk = MOBA_BLOCK
    q = q_ref[jj * blk:(jj + 1) * blk, :]
    qb = q.astype(BF16)
    qs = (q * (scale * LOG2_E)).astype(BF16)
    ln = (jj + 1) * blk
    nt = (((1,), (1,)), ((), ()))
    st = lax.dot_general(kb_ref[0:ln, :], qs, nt, preferred_element_type=F32)
    pieces = []
    if jj > MOBA_TOPK:
        g = lax.dot_general(km_ref[...].astype(BF16), qb, nt, preferred_element_type=F32)
        for n in range(jj):
            cnt = jnp.zeros((1, blk), F32)
            for m in range(jj):
                if m == n:
                    continue
                beats = (g[m:m + 1] >= g[n:n + 1]) if m < n else (g[m:m + 1] > g[n:n + 1])
                cnt = cnt + jnp.where(beats, 1.0, 0.0)
            pieces.append(jnp.where(cnt < MOBA_TOPK, st[n * blk:(n + 1) * blk, :], -jnp.inf))
    else:
        pieces = [st[n * blk:(n + 1) * blk, :] for n in range(jj)]
    key = lax.broadcasted_iota(jnp.int32, (blk, blk), 0)
    qry = lax.broadcasted_iota(jnp.int32, (blk, blk), 1)
    pieces.append(jnp.where(key <= qry, st[jj * blk:ln, :], -jnp.inf))
    mx = jnp.max(pieces[-1], axis=0, keepdims=True)
    for pc in pieces[:-1]:
        mx = jnp.maximum(mx, jnp.max(pc, axis=0, keepdims=True))
    ps = [jnp.exp2(pc - mx) for pc in pieces]
    den = ps[0].sum(axis=0, keepdims=True)
    for p in ps[1:]:
        den = den + p.sum(axis=0, keepdims=True)
    pcat = jnp.concatenate(ps, axis=0) if len(ps) > 1 else ps[0]
    acc_t = jnp.dot(vt_ref[:, 0:ln], pcat.astype(BF16), preferred_element_type=F32)
    o_ref[jj * blk:(jj + 1) * blk, :] = (acc_t / den).T.astype(o_ref.dtype)


def _attn_prompt_kernel(q_ref, k_ref, v_ref, *refs, nb, scale, n_cast):
    cast_src, refs = refs[:n_cast], refs[n_cast:]
    o_ref, cast_dst = refs[0], refs[1:1 + n_cast]
    kb_ref, vt_ref, km_ref = refs[1 + n_cast:]
    _cast_rows(cast_src, cast_dst)
    blk = MOBA_BLOCK
    kb_ref[...] = k_ref[...].astype(BF16)
    km_ref[...] = jnp.zeros(km_ref.shape, F32)
    for n in range(nb):
        vt_ref[:, n * blk:(n + 1) * blk] = v_ref[n * blk:(n + 1) * blk, :].T.astype(BF16)
        km_ref[n:n + 1, :] = jnp.mean(k_ref[n * blk:(n + 1) * blk, :], axis=0, keepdims=True)
    for jj in range(nb):
        _attn_block_body(jj, q_ref, kb_ref, vt_ref, km_ref, o_ref, scale)


def _attn_prompt(q, k, v, batch, seq, n_heads, hd, cast=(), layer=0):
    assert seq % MOBA_BLOCK == 0
    nb = seq // MOBA_BLOCK
    m = batch * seq
    bf16_rows = 2 * SUBLANES
    km_rows = -(-nb // bf16_rows) * bf16_rows
    kern = functools.partial(_attn_prompt_kernel, nb=nb, scale=hd ** -0.5, n_cast=len(cast))
    spec = pl.BlockSpec((seq, hd), lambda b, h: (b, h))
    c_in, c_out, c_shape, c_ops = _cast_stream(cast, layer, batch * n_heads, lambda b, h: b * n_heads + h)
    outs = pl.pallas_call(
        kern,
        grid=(batch, n_heads),
        in_specs=[spec, spec, spec] + c_in,
        out_specs=[spec] + c_out,
        out_shape=[jax.ShapeDtypeStruct((m, n_heads * hd), BF16)] + c_shape,
        scratch_shapes=[pltpu.VMEM((seq, hd), BF16), pltpu.VMEM((hd, seq), BF16), pltpu.VMEM((km_rows, hd), F32)],
        compiler_params=_params("parallel", "parallel"),
        name="attn_prompt",
    )(q, k, v, *c_ops)
    return outs[0] if not cast else outs


def _topk_kernel(km_ref, q_ref, o_ref):
    nblk = km_ref.shape[0]
    gate = jnp.sum(km_ref[...] * q_ref[...][None], axis=-1, keepdims=True)
    blk_id = lax.broadcasted_iota(jnp.int32, gate.shape, 0)
    for s in range(MOBA_TOPK):
        mx = jnp.max(gate, axis=0, keepdims=True)
        idx = jnp.min(jnp.where(gate == mx, blk_id, nblk), axis=0, keepdims=True)
        o_ref[s:s + 1] = idx
        gate = jnp.where(blk_id == idx, -jnp.inf, gate)


def _sample_topk(kmean, q):
    db, nblk, n_heads, hd = kmean.shape
    assert nblk >= MOBA_TOPK
    out = pl.pallas_call(
        _topk_kernel,
        grid=(db,),
        in_specs=[pl.BlockSpec((None, nblk, n_heads, hd), lambda b: (b, 0, 0, 0)),
                  pl.BlockSpec((None, n_heads, hd), lambda b: (b, 0, 0))],
        out_specs=pl.BlockSpec((None, MOBA_TOPK, n_heads, 1), lambda b: (b, 0, 0, 0)),
        out_shape=jax.ShapeDtypeStruct((db, MOBA_TOPK, n_heads, 1), jnp.int32),
        compiler_params=_params("parallel"),
        name="sample_topk",
    )(kmean, q)
    return jnp.swapaxes(out.reshape(db, MOBA_TOPK, n_heads), 1, 2)


def _attn_sample_kernel(pt_ref, sel_ref, q_ref, kn_ref, vn_ref, ck_hbm, cv_hbm, o_ref, kbuf, vbuf, sems, *,
                        layer, n_pages, ppb, scale):
    b = pl.program_id(0)
    n_heads, hd = q_ref.shape
    n_slab = MOBA_TOPK * ppb

    def slab_copies(h, slot):
        cps = []
        for s in range(MOBA_TOPK):
            blk = sel_ref[(b * n_heads + h) * MOBA_TOPK + s]
            for e in range(ppb):
                page = pt_ref[b * n_pages + blk * ppb + e]
                i = s * ppb + e
                cps.append(pltpu.make_async_copy(ck_hbm.at[layer, page, :, h, :], kbuf.at[slot, i], sems.at[0, slot, i]))
                cps.append(pltpu.make_async_copy(cv_hbm.at[layer, page, :, h, :], vbuf.at[slot, i], sems.at[1, slot, i]))
        return cps

    n_slot = kbuf.shape[0]
    group = n_slot // SAMPLE_GROUPS_BUFFERED
    n_key = n_slab * kbuf.shape[2]

    def group_copies(g):
        return [cp for h in range(g * group, (g + 1) * group) for cp in slab_copies(h, h % n_slot)]

    n_group = n_heads // group
    ahead = n_slot // group - 1
    for g0 in range(min(ahead, n_group)):
        for cp in group_copies(g0):
            cp.start()
    for h in range(n_heads):
        slot = h % n_slot
        if h % group == 0:
            g = h // group
            if g + ahead < n_group:
                for cp in group_copies(g + ahead):
                    cp.start()
            for cp in group_copies(g):
                cp.wait()
        q = q_ref[h:h + 1, :]
        qm = jnp.broadcast_to(q, (LANES, hd)).astype(BF16)
        keys = kbuf[slot].reshape(n_key, hd).astype(BF16)
        s = lax.dot_general(keys, qm, (((1,), (1,)), ((), ())), preferred_element_type=F32) * scale
        s_self = jnp.sum(q * kn_ref[h:h + 1, :], axis=1, keepdims=True) * scale
        mx = jnp.maximum(jnp.max(s, axis=0, keepdims=True), s_self)
        p = jnp.exp(s - mx)
        p_self = jnp.exp(s_self - mx)
        den = jnp.sum(p, axis=0, keepdims=True) + p_self
        acc = jnp.sum(p * vbuf[slot].reshape(n_key, hd), axis=0, keepdims=True) + p_self * vn_ref[h:h + 1, :]
        o_ref[h:h + 1, :] = acc / den


def _attn_sample(q, k_new, v_new, cache_k, cache_v, layer, pt_flat, sel_flat, n_pages, ppb):
    db, n_heads, hd = q.shape
    ps = cache_k.shape[2]
    n_slab = MOBA_TOPK * ppb
    n_slot = _tile(n_heads, SAMPLE_HEAD_GROUP) * SAMPLE_GROUPS_BUFFERED
    assert hd == LANES
    row_spec = pl.BlockSpec((None, n_heads, hd), lambda b, pt, sel: (b, 0, 0))
    any_spec = pl.BlockSpec(memory_space=pl.ANY)
    out = pl.pallas_call(
        functools.partial(_attn_sample_kernel, layer=layer, n_pages=n_pages, ppb=ppb, scale=hd ** -0.5),
        grid_spec=pltpu.PrefetchScalarGridSpec(
            num_scalar_prefetch=2,
            grid=(db,),
            in_specs=[row_spec, row_spec, row_spec, any_spec, any_spec],
            out_specs=row_spec,
            scratch_shapes=[pltpu.VMEM((n_slot, n_slab, ps, hd), F32), pltpu.VMEM((n_slot, n_slab, ps, hd), F32),
                            pltpu.SemaphoreType.DMA((2, n_slot, n_slab))],
        ),
        out_shape=jax.ShapeDtypeStruct((db, n_heads, hd), F32),
        compiler_params=_params("arbitrary"),
        name="attn_sample",
    )(pt_flat, sel_flat, q, k_new, v_new, cache_k, cache_v)
    return out.reshape(db, n_heads * hd)


def _in_projection(z, z_s, w_in, layer, c, a, rope_p, rope_s, table_rows, hd):
    m, ms = z.shape[0], z_s.shape[0]
    tm = _tile(m, 1024)
    tn_glu = _tile(c, 256)
    tn = _tile(a, 512)
    assert table_rows % tm == 0
    period = table_rows // tm
    tbl = [(t, (tm, hd), lambda j, i: (i % period, 0)) for t in rope_p]
    tbl_s = [(t, (ms, hd), lambda j, i: (0, 0)) for t in rope_s]
    col = lambda off, group=0: (0, w_in, layer, 0, off, group)
    u = _matmul([z], [col(0), col(c, 1)], [], _ep_glu, c, tn_glu, F32, "in_proj_glu", side=([z_s], []))
    q = _matmul([z], [col(2 * c)], tbl, _ep_rope, a, tn, F32, "in_proj_q", side=([z_s], tbl_s))
    k = _matmul([z], [col(2 * c + a)], tbl, _ep_rope, a, tn, F32, "in_proj_k", side=([z_s], tbl_s))
    v = _matmul([z], [col(2 * c + 2 * a)], [], _ep_plain, a, tn, F32, "in_proj_v", side=([z_s], []))
    return tuple(zip(u, q, k, v))


def _post_mixer(x, conv_out, attn, p, layer, w_out, g_ffn, w_up, w_down, g_ple, w_ple_gate, w_ple_proj, g_final,
                final_norm, paged=None):
    m, d = x.shape
    c = conv_out.shape[1]
    assert attn.shape[1] == c
    tm = _tile(m, 1024)
    tn = _tile(d, 512)
    resid = lambda r: (r, (tm, tn), lambda j, i: (i, j))
    h = _matmul([conv_out, attn], [(0, w_out, layer, 0, 0, 0), (1, w_out, layer, 1, 0, 0)], [resid(x)], _ep_resid,
                d, tn, F32, "out_proj")
    h = _mlp(h, g_ffn, w_up, w_down, paged)
    if paged is not None:
        h, kmean = h
    out = _ple(h, p, g_ple, g_final, w_ple_gate, w_ple_proj, layer, final_norm)
    return out if paged is None else (out, kmean)


def kernel(x_prompt, x_sample, cache_k, cache_v, state_conv, page_table, p_prompt, p_sample, g_mix, w_in, w_dw, b_dw, g_conv_ln, b_conv_ln, w_out, g_ffn, w_up, w_down, g_ple, w_ple_gate, w_ple_proj, g_final):
    batch, seq, d = x_prompt.shape
    db, dseq, _ = x_sample.shape
    assert dseq == 1
    depth, n_pool, page, n_heads, hd = cache_k.shape
    n_pages = page_table.shape[1]
    past_len = n_pages * page
    assert MOBA_BLOCK % page == 0 and past_len % MOBA_BLOCK == 0
    ppb = MOBA_BLOCK // page
    c = w_dw.shape[-1]
    a = n_heads * hd
    assert c + a == d

    hp = x_prompt.reshape(batch * seq, d)
    hs = x_sample.reshape(db, d)
    cos_p, sin_p = _rope_tables(0, seq, hd)
    cos_s, sin_s = _rope_tables(past_len, 1, hd)
    cos_s = jnp.broadcast_to(cos_s, (db, hd))
    sin_s = jnp.broadcast_to(sin_s, (db, hd))
    pt_flat = page_table.reshape(-1).astype(jnp.int32)

    kp_l, vp_l, cp_l, ks_l, vs_l, cs_l = [], [], [], [], [], []
    for l in range(depth):
        conv_w = (w_dw[l], b_dw[l], g_conv_ln[l], b_conv_ln[l])

        z = _rmsnorm(hp, g_mix[l], BF16)
        z_s = _rmsnorm(hs, g_mix[l], BF16)
        (u, q, k, v), (u_s, q_s, k_s, v_s) = _in_projection(z, z_s, w_in, l, c, a, (cos_p, sin_p), (cos_s, sin_s),
                                                            seq, hd)

        conv_out, w_down_b, w_pg_b = _conv_prompt(u, batch, seq, *conv_w, cast=(w_down, w_ple_gate), layer=l)
        attn, w_up_b = _attn_prompt(q, k, v, batch, seq, n_heads, hd, cast=(w_up,), layer=l)
        post = (l, w_out, g_ffn[l], w_up_b, w_down_b, g_ple[l], w_pg_b, w_ple_proj, g_final, l == depth - 1)
        hp, kmean = _post_mixer(hp, conv_out, attn, p_prompt[l].reshape(batch * seq, -1), *post,
                                paged=(cache_k, l, pt_flat, ppb))
        kp_l.append(k.reshape(batch, seq, n_heads, hd))
        vp_l.append(v.reshape(batch, seq, n_heads, hd))
        cp_l.append(u.reshape(batch, seq, c)[:, seq - (w_dw.shape[1] - 1):, :])

        conv_out, new_state = _conv_sample(u_s, state_conv[l], *conv_w)
        h3 = lambda t: t.reshape(db, n_heads, hd)
        sel = _sample_topk(kmean.reshape(db, n_pages // ppb, n_heads, hd), h3(q_s))
        attn = _attn_sample(h3(q_s), h3(k_s), h3(v_s), cache_k, cache_v, l, pt_flat, sel.reshape(-1), n_pages, ppb)
        hs = _post_mixer(hs, conv_out.astype(BF16), attn.astype(BF16), p_sample[l].reshape(db, -1), *post)
        ks_l.append(k_s.reshape(db, 1, n_heads, hd))
        vs_l.append(v_s.reshape(db, 1, n_heads, hd))
        cs_l.append(new_state)

    y_prompt = hp.reshape(batch, seq, d)
    y_sample = hs.reshape(db, 1, d)
    stack = lambda xs: xs[0][None] if len(xs) == 1 else jnp.stack(xs)
    return (y_prompt, y_sample, stack(kp_l), stack(vp_l), stack(cp_l), stack(ks_l), stack(vs_l), stack(cs_l))
```

```python
import functools

import jax
import jax.numpy as jnp
from jax import lax
from jax.experimental import pallas as pl
from jax.experimental.pallas import tpu as pltpu

F32 = jnp.float32
BF16 = jnp.bfloat16

MOBA_BLOCK = 256
MOBA_TOPK = 3
ROPE_THETA = 10000.0
NORM_EPS = 1e-6
LOG2_E = 1.4426950408889634
LANES = 128
SUBLANES = 8
CONV_HALO = 32
MM_ROW_CHUNK = 512
NORM_ROW_CHUNK = 64
SAMPLE_HEAD_GROUP = 4
SAMPLE_GROUPS_BUFFERED = 3
VMEM_LIMIT = 60 * 1024 * 1024


def _params(*sem):
    return pltpu.CompilerParams(dimension_semantics=sem, vmem_limit_bytes=VMEM_LIMIT)


def _tile(n, pref):
    if n <= pref:
        return n
    t = pref
    while n % t:
        t //= 2
    return t


def _rmsnorm_kernel(x_ref, g_ref, o_ref):
    x = x_ref[...]
    ms = jnp.mean(x * x, axis=-1, keepdims=True)
    o_ref[...] = (x * lax.rsqrt(ms + NORM_EPS) * g_ref[...]).astype(o_ref.dtype)


def _rmsnorm(x, g, out_dtype):
    m, d = x.shape
    tm = _tile(m, 256)
    return pl.pallas_call(
        _rmsnorm_kernel,
        grid=(m // tm,),
        in_specs=[pl.BlockSpec((tm, d), lambda i: (i, 0)), pl.BlockSpec((1, d), lambda i: (0, 0))],
        out_specs=pl.BlockSpec((tm, d), lambda i: (i, 0)),
        out_shape=jax.ShapeDtypeStruct((m, d), out_dtype),
        compiler_params=_params("parallel"),
        name="rmsnorm",
    )(x, g.reshape(1, d))


def _mm_kernel(*refs, n_lhs, dots, n_extra, epilogue, side):
    n_dot = len(dots)
    groups = 2 if side else 1
    pos = 0
    lhs = [refs[pos + g * n_lhs:pos + (g + 1) * n_lhs] for g in range(groups)]
    pos += groups * n_lhs
    ws = refs[pos:pos + n_dot]
    pos += n_dot
    extras = [refs[pos + g * n_extra:pos + (g + 1) * n_extra] for g in range(groups)]
    pos += groups * n_extra
    outs = refs[pos:pos + groups]
    wbs = refs[pos + groups:pos + groups + n_dot]
    i = pl.program_id(1)

    @pl.when(i == 0)
    def _():
        for w, wb in zip(ws, wbs):
            wb[...] = w[...].astype(BF16)

    def run(g):
        rows = outs[g].shape[0]
        rc = MM_ROW_CHUNK if rows % MM_ROW_CHUNK == 0 else rows
        for r0 in range(0, rows, rc):
            accs = {}
            for (li, group), wb in zip(dots, wbs):
                a = lhs[g][li][r0:r0 + rc, :]
                if a.dtype != BF16:
                    a = a.astype(BF16)
                d = jnp.dot(a, wb[...], preferred_element_type=F32)
                accs[group] = d if group not in accs else accs[group] + d
            res = epilogue([accs[k] for k in sorted(accs)], [e[r0:r0 + rc, :] for e in extras[g]])
            outs[g][r0:r0 + rc, :] = res.astype(outs[g].dtype)

    run(0)
    if side:
        pl.when(i == pl.num_programs(1) - 1)(functools.partial(run, 1))


def _matmul(lhs, dots, extras, epilogue, n_cols, tn, out_dtype, name, side=None, tm_pref=1024):
    m = lhs[0].shape[0]
    tm = _tile(m, tm_pref)
    in_specs = [pl.BlockSpec((tm, a.shape[1]), lambda j, i: (i, 0)) for a in lhs]
    operands = list(lhs)
    if side:
        lhs_s, extras_s = side
        ms = lhs_s[0].shape[0]
        in_specs += [pl.BlockSpec((ms, a.shape[1]), lambda j, i: (0, 0)) for a in lhs_s]
        operands += list(lhs_s)
    scratch = []
    for li, w, layer, rb, co, _ in dots:
        kl = lhs[li].shape[1]
        assert co % tn == 0
        in_specs.append(pl.BlockSpec(
            (None, kl, tn), functools.partial(lambda j, i, layer, rb, cb: (layer, rb, cb + j), layer=layer, rb=rb,
                                              cb=co // tn)))
        operands.append(w)
        scratch.append(pltpu.VMEM((kl, tn), BF16))
    for arr, blk, imap in list(extras) + (list(extras_s) if side else []):
        in_specs.append(pl.BlockSpec(blk, imap))
        operands.append(arr)
    out_specs = [pl.BlockSpec((tm, tn), lambda j, i: (i, j))]
    out_shape = [jax.ShapeDtypeStruct((m, n_cols), out_dtype)]
    if side:
        out_specs.append(pl.BlockSpec((ms, tn), lambda j, i: (0, j)))
        out_shape.append(jax.ShapeDtypeStruct((ms, n_cols), out_dtype))
    kern = functools.partial(_mm_kernel, n_lhs=len(lhs), dots=[(d[0], d[5]) for d in dots],
                             n_extra=len(extras), epilogue=epilogue, side=bool(side))
    outs = pl.pallas_call(
        kern,
        grid=(n_cols // tn, m // tm),
        in_specs=in_specs,
        out_specs=out_specs,
        out_shape=out_shape,
        scratch_shapes=scratch,
        compiler_params=_params("parallel", "arbitrary"),
        name=name,
    )(*operands)
    return outs if side else outs[0]


def _ep_plain(accs, extras):
    return accs[0]


def _ep_glu(accs, extras):
    return accs[0] * jax.nn.sigmoid(accs[1])


def _ep_resid(accs, extras):
    return extras[0] + accs[0]


def _ep_rope(accs, extras):
    cos, sin_signed = extras
    y = accs[0]
    hd = cos.shape[1]
    outs = []
    for h in range(y.shape[1] // hd):
        xh = y[:, h * hd:(h + 1) * hd]
        outs.append(xh * cos + pltpu.roll(xh, hd // 2, 1) * sin_signed)
    return jnp.concatenate(outs, axis=1)


def _rope_tables(pos0, s, hd):
    inv = 1.0 / (ROPE_THETA ** (jnp.arange(0, hd, 2, dtype=F32) / hd))
    pos = (jnp.arange(s, dtype=jnp.int32) + pos0).astype(F32)
    ang = pos[:, None] * inv[None, :]
    cos, sin = jnp.cos(ang), jnp.sin(ang)
    return jnp.concatenate([cos, cos], axis=-1), jnp.concatenate([-sin, sin], axis=-1)


def _rms_scale(x, g):
    ms = jnp.mean(x * x, axis=-1, keepdims=True)
    return x * lax.rsqrt(ms + NORM_EPS) * g


def _mlp_kernel(pt_ref, h_ref, g_ref, wu_ref, wd_ref, *refs, n_page, ppb):
    page_refs = refs[:n_page]
    o_ref = refs[n_page]
    z_ref = refs[-1]
    f = pl.program_id(1)

    @pl.when(f == 0)
    def _():
        rows = h_ref.shape[0]
        rc = _tile(rows, NORM_ROW_CHUNK)
        for r0 in range(0, rows, rc):
            h = h_ref[r0:r0 + rc, :]
            o_ref[r0:r0 + rc, :] = h
            z_ref[r0:r0 + rc, :] = _rms_scale(h, g_ref[...]).astype(BF16)

    hm = jnp.dot(z_ref[...], wu_ref[...], preferred_element_type=F32)
    hm = jnp.square(jnp.maximum(hm, 0.0)).astype(BF16)
    o_ref[...] += jnp.dot(hm, wd_ref[...], preferred_element_type=F32)

    if n_page:
        km_ref = refs[n_page + 1]
        ps = page_refs[0].shape[0]
        for r in range(n_page // ppb):
            s = jnp.sum(page_refs[r * ppb][...], axis=0)
            for e in range(1, ppb):
                s = s + jnp.sum(page_refs[r * ppb + e][...], axis=0)
            km_ref[r] = s * (1.0 / (ppb * ps))


def _mlp(h, g, w_up, w_down, paged=None):
    m, d = h.shape
    dff = w_up.shape[1]
    tm = _tile(m, 512)
    tf = _tile(dff, 512)
    nf = dff // tf
    steps = (m // tm) * nf
    in_specs = [
        pl.BlockSpec((tm, d), lambda i, f, pt: (i, 0), pipeline_mode=pl.Buffered(1)),
        pl.BlockSpec((1, d), lambda i, f, pt: (0, 0)),
        pl.BlockSpec((d, tf), lambda i, f, pt: (0, f)),
        pl.BlockSpec((tf, d), lambda i, f, pt: (f, 0)),
    ]
    out_specs = [pl.BlockSpec((tm, d), lambda i, f, pt: (i, 0))]
    out_shape = [jax.ShapeDtypeStruct((m, d), F32)]
    operands = [h, g.reshape(1, d), w_up, w_down]
    n_page, ppb = 0, 1
    if paged is None:
        pt_flat = jnp.zeros((1,), jnp.int32)
    else:
        cache_k, layer, pt_flat, ppb = paged
        _, _, ps, n_heads, hd = cache_k.shape
        n_slots = pt_flat.shape[0]
        assert n_slots % (steps * ppb) == 0
        n_page = n_slots // steps
        for e in range(n_page):
            in_specs.append(pl.BlockSpec(
                (None, None, ps, n_heads, hd),
                functools.partial(lambda i, f, pt, e: (layer, pt[(i * nf + f) * n_page + e], 0, 0, 0), e=e)))
        operands += [cache_k] * n_page
        out_specs.append(pl.BlockSpec((n_page // ppb, n_heads, hd), lambda i, f, pt: (i * nf + f, 0, 0)))
        out_shape.append(jax.ShapeDtypeStruct((n_slots // ppb, n_heads, hd), F32))
    outs = pl.pallas_call(
        functools.partial(_mlp_kernel, n_page=n_page, ppb=ppb),
        grid_spec=pltpu.PrefetchScalarGridSpec(
            num_scalar_prefetch=1, grid=(m // tm, nf), in_specs=in_specs, out_specs=out_specs,
            scratch_shapes=[pltpu.VMEM((tm, d), BF16)]),
        out_shape=out_shape,
        compiler_params=_params("parallel", "arbitrary"),
        name="mlp",
    )(pt_flat, *operands)
    return outs[0] if paged is None else outs


def _ple_kernel(h_ref, p_ref, g_ref, gf_ref, wg_ref, wp_ref, y_ref, z_ref, pb_ref, *, final_norm):
    j = pl.program_id(1)
    tn = wg_ref.shape[1]

    rows = h_ref.shape[0]
    rc = _tile(rows, NORM_ROW_CHUNK)

    @pl.when(j == 0)
    def _():
        for r0 in range(0, rows, rc):
            z_ref[r0:r0 + rc, :] = _rms_scale(h_ref[r0:r0 + rc, :], g_ref[...]).astype(BF16)
        pb_ref[...] = p_ref[...].astype(BF16)

    cols = pl.ds(pl.multiple_of(j * tn, tn), tn)
    gate = jnp.dot(z_ref[...], wg_ref[...], preferred_element_type=F32)
    emb = jnp.dot(pb_ref[...], wp_ref[...].astype(BF16), preferred_element_type=F32)
    y_ref[:, cols] = h_ref[:, cols] + jax.nn.sigmoid(gate) * emb

    if final_norm:
        @pl.when(j == pl.num_programs(1) - 1)
        def _():
            for r0 in range(0, rows, rc):
                y_ref[r0:r0 + rc, :] = _rms_scale(y_ref[r0:r0 + rc, :], gf_ref[...])


def _ple(h, p, g_ple, g_final, w_gate, w_proj, layer, final_norm):
    m, d = h.shape
    pdim = p.shape[1]
    tm = _tile(m, 512)
    tn = _tile(d, 1024)
    vec = pl.BlockSpec((1, d), lambda i, j: (0, 0))
    return pl.pallas_call(
        functools.partial(_ple_kernel, final_norm=final_norm),
        grid=(m // tm, d // tn),
        in_specs=[
            pl.BlockSpec((tm, d), lambda i, j: (i, 0), pipeline_mode=pl.Buffered(1)),
            pl.BlockSpec((tm, pdim), lambda i, j: (i, 0)),
            vec, vec,
            pl.BlockSpec((d, tn), lambda i, j: (0, j)),
            pl.BlockSpec((None, pdim, tn), lambda i, j: (layer, 0, j)),
        ],
        out_specs=pl.BlockSpec((tm, d), lambda i, j: (i, 0)),
        out_shape=jax.ShapeDtypeStruct((m, d), F32),
        scratch_shapes=[pltpu.VMEM((tm, d), BF16), pltpu.VMEM((tm, pdim), BF16)],
        compiler_params=_params("parallel", "arbitrary"),
        name="ple",
    )(h, p, g_ple.reshape(1, d), g_final.reshape(1, d), w_gate, w_proj)


def _cast_stream(weights, layer, steps, step_of):
    in_specs, out_specs, out_shapes = [], [], []
    for w in weights:
        _, r, c = w.shape
        rows = r // steps
        assert r % steps == 0 and rows % (2 * SUBLANES) == 0
        in_specs.append(pl.BlockSpec((None, rows, c), lambda *g: (layer, step_of(*g), 0)))
        out_specs.append(pl.BlockSpec((rows, c), lambda *g: (step_of(*g), 0)))
        out_shapes.append(jax.ShapeDtypeStruct((r, c), BF16))
    return in_specs, out_specs, out_shapes, list(weights)


def _cast_rows(src_refs, dst_refs):
    for src, dst in zip(src_refs, dst_refs):
        dst[...] = src[...].astype(BF16)


def _ln_swish(y, g, beta):
    mu = jnp.mean(y, axis=-1, keepdims=True)
    xc = y - mu
    var = jnp.mean(xc * xc, axis=-1, keepdims=True)
    yn = xc * lax.rsqrt(var + NORM_EPS) * g + beta
    return yn * jax.nn.sigmoid(yn)


def _conv_prompt_kernel(cur_ref, halo_ref, w_ref, b_ref, g_ref, beta_ref, *refs, kw, cc, n_cast):
    cast_src, refs = refs[:n_cast], refs[n_cast:]
    o_ref, cast_dst = refs[0], refs[1:1 + n_cast]
    ext_ref, sh_ref, y_ref = refs[1 + n_cast:]
    _cast_rows(cast_src, cast_dst)
    t = pl.program_id(1)
    tt, c = cur_ref.shape
    ext_ref[0:CONV_HALO, :] = jnp.where(t == 0, 0.0, halo_ref[...])
    ext_ref[CONV_HALO:CONV_HALO + tt, :] = cur_ref[...]
    off = CONV_HALO - (kw - 1)
    rows = sh_ref.shape[1]
    for phase in range(1, SUBLANES):
        sh_ref[phase - 1] = ext_ref[phase:phase + rows, :]
    groups = tt // SUBLANES

    def column_chunk(ci, carry):
        cols = pl.ds(pl.multiple_of(ci * cc, cc), cc)
        acc = None
        for k in range(kw):
            phase = (off + k) % SUBLANES
            a0 = off + k - phase
            src = ext_ref if phase == 0 else sh_ref.at[phase - 1]
            term = src[a0:a0 + tt, cols].reshape(groups, SUBLANES, cc) * w_ref[k, :, cols][None]
            acc = term if acc is None else acc + term
        y_ref[:, cols] = acc.reshape(tt, cc) + b_ref[:, cols]
        return carry

    lax.fori_loop(0, c // cc, column_chunk, 0)
    rc = 2 * SUBLANES
    for r0 in range(0, tt, rc):
        o_ref[r0:r0 + rc, :] = _ln_swish(y_ref[r0:r0 + rc, :], g_ref[...], beta_ref[...]).astype(o_ref.dtype)


def _conv_prompt(u, batch, seq, w_dw, b_dw, ln_g, ln_b, cast=(), layer=0):
    m, c = u.shape
    kw = w_dw.shape[0]
    assert kw - 1 <= CONV_HALO
    tt = _tile(seq, 128)
    assert tt % CONV_HALO == 0
    nt = seq // tt
    hb = tt // CONV_HALO
    vec = lambda a: a.reshape(1, c)
    vspec = pl.BlockSpec((1, c), lambda b, t: (0, 0))
    kern = functools.partial(_conv_prompt_kernel, kw=kw, cc=min(LANES, c), n_cast=len(cast))
    w_rep = jnp.broadcast_to(w_dw[:, None, :], (kw, SUBLANES, c))
    c_in, c_out, c_shape, c_ops = _cast_stream(cast, layer, batch * nt, lambda b, t: b * nt + t)
    outs = pl.pallas_call(
        kern,
        grid=(batch, nt),
        in_specs=[
            pl.BlockSpec((tt, c), lambda b, t: (b * nt + t, 0)),
            pl.BlockSpec((CONV_HALO, c), lambda b, t: (jnp.maximum((b * nt + t) * hb - 1, 0), 0)),
            pl.BlockSpec((kw, SUBLANES, c), lambda b, t: (0, 0, 0)),
            vspec, vspec, vspec,
        ] + c_in,
        out_specs=[pl.BlockSpec((tt, c), lambda b, t: (b * nt + t, 0))] + c_out,
        out_shape=[jax.ShapeDtypeStruct((m, c), BF16)] + c_shape,
        scratch_shapes=[pltpu.VMEM((CONV_HALO + tt, c), F32),
                        pltpu.VMEM((SUBLANES - 1, CONV_HALO + tt - SUBLANES, c), F32),
                        pltpu.VMEM((tt, c), F32)],
        compiler_params=_params("parallel", "parallel"),
        name="conv_prompt",
    )(u, u, w_rep, vec(b_dw), vec(ln_g), vec(ln_b), *c_ops)
    return outs[0] if not cast else outs


def _conv_sample_kernel(st_ref, u_ref, w_ref, b_ref, g_ref, beta_ref, o_ref, ns_ref):
    ks = st_ref.shape[0]
    st = st_ref[...]
    u = u_ref[...]
    y = jnp.sum(st * w_ref[0:ks, :], axis=0, keepdims=True) + u * w_ref[ks:ks + 1, :] + b_ref[...]
    o_ref[...] = _ln_swish(y, g_ref[...], beta_ref[...]).astype(o_ref.dtype)
    ns_ref[0:ks - 1, :] = st_ref[1:ks, :]
    ns_ref[ks - 1:ks, :] = u


def _conv_sample(u, state, w_dw, b_dw, ln_g, ln_b):
    db, c = u.shape
    ks = state.shape[1]
    vec = lambda a: a.reshape(1, c)
    vspec = pl.BlockSpec((1, c), lambda b: (0, 0))
    out, new_state = pl.pallas_call(
        _conv_sample_kernel,
        grid=(db,),
        in_specs=[
            pl.BlockSpec((None, ks, c), lambda b: (b, 0, 0)),
            pl.BlockSpec((None, 1, c), lambda b: (b, 0, 0)),
            pl.BlockSpec((ks + 1, c), lambda b: (0, 0)),
            vspec, vspec, vspec,
        ],
        out_specs=[pl.BlockSpec((None, 1, c), lambda b: (b, 0, 0)), pl.BlockSpec((None, ks, c), lambda b: (b, 0, 0))],
        out_shape=[jax.ShapeDtypeStruct((db, 1, c), F32), jax.ShapeDtypeStruct((db, ks, c), F32)],
        compiler_params=_params("parallel"),
        name="conv_sample",
    )(state, u.reshape(db, 1, c), w_dw, vec(b_dw), vec(ln_g), vec(ln_b))
    return out.reshape(db, c), new_state


def _attn_block_body(jj, q_ref, kb_ref, vt_ref, km_ref, o_ref, scale):
    blk = MOBA_BLOCK
    q = q_ref[jj * blk:(jj + 1) * blk, :]
    qb = q.astype(BF16)
    qs = (q * (scale * LOG2_E)).astype(BF16)
    ln = (jj + 1) * blk
    nt = (((1,), (1,)), ((), ()))
    st = lax.dot_general(kb_ref[0:ln, :], qs, nt, preferred_element_type=F32)
    pieces = []
    if jj > MOBA_TOPK:
        g = lax.dot_general(km_ref[...].astype(BF16), qb, nt, preferred_element_type=F32)
        for n in range(jj):
            cnt = jnp.zeros((1, blk), F32)
            for m in range(jj):
                if m == n:
                    continue
                beats = (g[m:m + 1] >= g[n:n + 1]) if m < n else (g[m:m + 1] > g[n:n + 1])
                cnt = cnt + jnp.where(beats, 1.0, 0.0)
            pieces.append(jnp.where(cnt < MOBA_TOPK, st[n * blk:(n + 1) * blk, :], -jnp.inf))
    else:
        pieces = [st[n * blk:(n + 1) * blk, :] for n in range(jj)]
    key = lax.broadcasted_iota(jnp.int32, (blk, blk), 0)
    qry = lax.broadcasted_iota(jnp.int32, (blk, blk), 1)
    pieces.append(jnp.where(key <= qry, st[jj * blk:ln, :], -jnp.inf))
    mx = jnp.max(pieces[-1], axis=0, keepdims=True)
    for pc in pieces[:-1]:
        mx = jnp.maximum(mx, jnp.max(pc, axis=0, keepdims=True))
    ps = [jnp.exp2(pc - mx) for pc in pieces]
    den = ps[0].sum(axis=0, keepdims=True)
    for p in ps[1:]:
        den = den + p.sum(axis=0, keepdims=True)
    pcat = jnp.concatenate(ps, axis=0) if len(ps) > 1 else ps[0]
    acc_t = jnp.dot(vt_ref[:, 0:ln], pcat.astype(BF16), preferred_element_type=F32)
    o_ref[jj * blk:(jj + 1) * blk, :] = (acc_t / den).T.astype(o_ref.dtype)


def _attn_prompt_kernel(q_ref, k_ref, v_ref, *refs, nb, scale, n_cast):
    cast_src, refs = refs[:n_cast], refs[n_cast:]
    o_ref, cast_dst = refs[0], refs[1:1 + n_cast]
    kb_ref, vt_ref, km_ref = refs[1 + n_cast:]
    _cast_rows(cast_src, cast_dst)
    blk = MOBA_BLOCK
    kb_ref[...] = k_ref[...].astype(BF16)
    km_ref[...] = jnp.zeros(km_ref.shape, F32)
    for n in range(nb):
        vt_ref[:, n * blk:(n + 1) * blk] = v_ref[n * blk:(n + 1) * blk, :].T.astype(BF16)
        km_ref[n:n + 1, :] = jnp.mean(k_ref[n * blk:(n + 1) * blk, :], axis=0, keepdims=True)
    for jj in range(nb):
        _attn_block_body(jj, q_ref, kb_ref, vt_ref, km_ref, o_ref, scale)


def _attn_prompt(q, k, v, batch, seq, n_heads, hd, cast=(), layer=0):
    assert seq % MOBA_BLOCK == 0
    nb = seq // MOBA_BLOCK
    m = batch * seq
    bf16_rows = 2 * SUBLANES
    km_rows = -(-nb // bf16_rows) * bf16_rows
    kern = functools.partial(_attn_prompt_kernel, nb=nb, scale=hd ** -0.5, n_cast=len(cast))
    spec = pl.BlockSpec((seq, hd), lambda b, h: (b, h))
    c_in, c_out, c_shape, c_ops = _cast_stream(cast, layer, batch * n_heads, lambda b, h: b * n_heads + h)
    outs = pl.pallas_call(
        kern,
        grid=(batch, n_heads),
        in_specs=[spec, spec, spec] + c_in,
        out_specs=[spec] + c_out,
        out_shape=[jax.ShapeDtypeStruct((m, n_heads * hd), BF16)] + c_shape,
        scratch_shapes=[pltpu.VMEM((seq, hd), BF16), pltpu.VMEM((hd, seq), BF16), pltpu.VMEM((km_rows, hd), F32)],
        compiler_params=_params("parallel", "parallel"),
        name="attn_prompt",
    )(q, k, v, *c_ops)
    return outs[0] if not cast else outs


def _topk_kernel(km_ref, q_ref, o_ref):
    nblk = km_ref.shape[0]
    gate = jnp.sum(km_ref[...] * q_ref[...][None], axis=-1, keepdims=True)
    blk_id = lax.broadcasted_iota(jnp.int32, gate.shape, 0)
    for s in range(MOBA_TOPK):
        mx = jnp.max(gate, axis=0, keepdims=True)
        idx = jnp.min(jnp.where(gate == mx, blk_id, nblk), axis=0, keepdims=True)
        o_ref[s:s + 1] = idx
        gate = jnp.where(blk_id == idx, -jnp.inf, gate)


def _sample_topk(kmean, q):
    db, nblk, n_heads, hd = kmean.shape
    assert nblk >= MOBA_TOPK
    out = pl.pallas_call(
        _topk_kernel,
        grid=(db,),
        in_specs=[pl.BlockSpec((None, nblk, n_heads, hd), lambda b: (b, 0, 0, 0)),
                  pl.BlockSpec((None, n_heads, hd), lambda b: (b, 0, 0))],
        out_specs=pl.BlockSpec((None, MOBA_TOPK, n_heads, 1), lambda b: (b, 0, 0, 0)),
        out_shape=jax.ShapeDtypeStruct((db, MOBA_TOPK, n_heads, 1), jnp.int32),
        compiler_params=_params("parallel"),
        name="sample_topk",
    )(kmean, q)
    return jnp.swapaxes(out.reshape(db, MOBA_TOPK, n_heads), 1, 2)


def _attn_sample_kernel(pt_ref, sel_ref, q_ref, kn_ref, vn_ref, ck_hbm, cv_hbm, o_ref, kbuf, vbuf, sems, *,
                        layer, n_pages, ppb, scale):
    b = pl.program_id(0)
    n_heads, hd = q_ref.shape
    n_slab = MOBA_TOPK * ppb

    def slab_copies(h, slot):
        cps = []
        for s in range(MOBA_TOPK):
            blk = sel_ref[(b * n_heads + h) * MOBA_TOPK + s]
            for e in range(ppb):
                page = pt_ref[b * n_pages + blk * ppb + e]
                i = s * ppb + e
                cps.append(pltpu.make_async_copy(ck_hbm.at[layer, page, :, h, :], kbuf.at[slot, i], sems.at[0, slot, i]))
                cps.append(pltpu.make_async_copy(cv_hbm.at[layer, page, :, h, :], vbuf.at[slot, i], sems.at[1, slot, i]))
        return cps

    n_slot = kbuf.shape[0]
    group = n_slot // SAMPLE_GROUPS_BUFFERED
    n_key = n_slab * kbuf.shape[2]

    def group_copies(g):
        return [cp for h in range(g * group, (g + 1) * group) for cp in slab_copies(h, h % n_slot)]

    n_group = n_heads // group
    ahead = n_slot // group - 1
    for g0 in range(min(ahead, n_group)):
        for cp in group_copies(g0):
            cp.start()
    for h in range(n_heads):
        slot = h % n_slot
        if h % group == 0:
            g = h // group
            if g + ahead < n_group:
                for cp in group_copies(g + ahead):
                    cp.start()
            for cp in group_copies(g):
                cp.wait()
        q = q_ref[h:h + 1, :]
        qm = jnp.broadcast_to(q, (LANES, hd)).astype(BF16)
        keys = kbuf[slot].reshape(n_key, hd).astype(BF16)
        s = lax.dot_general(keys, qm, (((1,), (1,)), ((), ())), preferred_element_type=F32) * scale
        s_self = jnp.sum(q * kn_ref[h:h + 1, :], axis=1, keepdims=True) * scale
        mx = jnp.maximum(jnp.max(s, axis=0, keepdims=True), s_self)
        p = jnp.exp(s - mx)
        p_self = jnp.exp(s_self - mx)
        den = jnp.sum(p, axis=0, keepdims=True) + p_self
        acc = jnp.sum(p * vbuf[slot].reshape(n_key, hd), axis=0, keepdims=True) + p_self * vn_ref[h:h + 1, :]
        o_ref[h:h + 1, :] = acc / den


def _attn_sample(q, k_new, v_new, cache_k, cache_v, layer, pt_flat, sel_flat, n_pages, ppb):
    db, n_heads, hd = q.shape
    ps = cache_k.shape[2]
    n_slab = MOBA_TOPK * ppb
    n_slot = _tile(n_heads, SAMPLE_HEAD_GROUP) * SAMPLE_GROUPS_BUFFERED
    assert hd == LANES
    row_spec = pl.BlockSpec((None, n_heads, hd), lambda b, pt, sel: (b, 0, 0))
    any_spec = pl.BlockSpec(memory_space=pl.ANY)
    out = pl.pallas_call(
        functools.partial(_attn_sample_kernel, layer=layer, n_pages=n_pages, ppb=ppb, scale=hd ** -0.5),
        grid_spec=pltpu.PrefetchScalarGridSpec(
            num_scalar_prefetch=2,
            grid=(db,),
            in_specs=[row_spec, row_spec, row_spec, any_spec, any_spec],
            out_specs=row_spec,
            scratch_shapes=[pltpu.VMEM((n_slot, n_slab, ps, hd), F32), pltpu.VMEM((n_slot, n_slab, ps, hd), F32),
                            pltpu.SemaphoreType.DMA((2, n_slot, n_slab))],
        ),
        out_shape=jax.ShapeDtypeStruct((db, n_heads, hd), F32),
        compiler_params=_params("arbitrary"),
        name="attn_sample",
    )(pt_flat, sel_flat, q, k_new, v_new, cache_k, cache_v)
    return out.reshape(db, n_heads * hd)


def _in_projection(z, z_s, w_in, layer, c, a, rope_p, rope_s, table_rows, hd):
    m, ms = z.shape[0], z_s.shape[0]
    tm = _tile(m, 1024)
    tn_glu = _tile(c, 256)
    tn = _tile(a, 512)
    assert table_rows % tm == 0
    period = table_rows // tm
    tbl = [(t, (tm, hd), lambda j, i: (i % period, 0)) for t in rope_p]
    tbl_s = [(t, (ms, hd), lambda j, i: (0, 0)) for t in rope_s]
    col = lambda off, group=0: (0, w_in, layer, 0, off, group)
    u = _matmul([z], [col(0), col(c, 1)], [], _ep_glu, c, tn_glu, F32, "in_proj_glu", side=([z_s], []))
    q = _matmul([z], [col(2 * c)], tbl, _ep_rope, a, tn, F32, "in_proj_q", side=([z_s], tbl_s))
    k = _matmul([z], [col(2 * c + a)], tbl, _ep_rope, a, tn, F32, "in_proj_k", side=([z_s], tbl_s))
    v = _matmul([z], [col(2 * c + 2 * a)], [], _ep_plain, a, tn, F32, "in_proj_v", side=([z_s], []))
    return tuple(zip(u, q, k, v))


def _post_mixer(x, conv_out, attn, p, layer, w_out, g_ffn, w_up, w_down, g_ple, w_ple_gate, w_ple_proj, g_final,
                final_norm, paged=None):
    m, d = x.shape
    c = conv_out.shape[1]
    assert attn.shape[1] == c
    tm = _tile(m, 1024)
    tn = _tile(d, 512)
    resid = lambda r: (r, (tm, tn), lambda j, i: (i, j))
    h = _matmul([conv_out, attn], [(0, w_out, layer, 0, 0, 0), (1, w_out, layer, 1, 0, 0)], [resid(x)], _ep_resid,
                d, tn, F32, "out_proj")
    h = _mlp(h, g_ffn, w_up, w_down, paged)
    if paged is not None:
        h, kmean = h
    out = _ple(h, p, g_ple, g_final, w_ple_gate, w_ple_proj, layer, final_norm)
    return out if paged is None else (out, kmean)


def kernel(x_prompt, x_sample, cache_k, cache_v, state_conv, page_table, p_prompt, p_sample, g_mix, w_in, w_dw, b_dw, g_conv_ln, b_conv_ln, w_out, g_ffn, w_up, w_down, g_ple, w_ple_gate, w_ple_proj, g_final):
    batch, seq, d = x_prompt.shape
    db, dseq, _ = x_sample.shape
    assert dseq == 1
    depth, n_pool, page, n_heads, hd = cache_k.shape
    n_pages = page_table.shape[1]
    past_len = n_pages * page
    assert MOBA_BLOCK % page == 0 and past_len % MOBA_BLOCK == 0
    ppb = MOBA_BLOCK // page
    c = w_dw.shape[-1]
    a = n_heads * hd
    assert c + a == d

    hp = x_prompt.reshape(batch * seq, d)
    hs = x_sample.reshape(db, d)
    cos_p, sin_p = _rope_tables(0, seq, hd)
    cos_s, sin_s = _rope_tables(past_len, 1, hd)
    cos_s = jnp.broadcast_to(cos_s, (db, hd))
    sin_s = jnp.broadcast_to(sin_s, (db, hd))
    pt_flat = page_table.reshape(-1).astype(jnp.int32)

    kp_l, vp_l, cp_l, ks_l, vs_l, cs_l = [], [], [], [], [], []
    for l in range(depth):
        conv_w = (w_dw[l], b_dw[l], g_conv_ln[l], b_conv_ln[l])

        z = _rmsnorm(hp, g_mix[l], BF16)
        z_s = _rmsnorm(hs, g_mix[l], BF16)
        (u, q, k, v), (u_s, q_s, k_s, v_s) = _in_projection(z, z_s, w_in, l, c, a, (cos_p, sin_p), (cos_s, sin_s),
                                                            seq, hd)

        conv_out, w_down_b, w_pg_b = _conv_prompt(u, batch, seq, *conv_w, cast=(w_down, w_ple_gate), layer=l)
        attn, w_up_b = _attn_prompt(q, k, v, batch, seq, n_heads, hd, cast=(w_up,), layer=l)
        post = (l, w_out, g_ffn[l], w_up_b, w_down_b, g_ple[l], w_pg_b, w_ple_proj, g_final, l == depth - 1)
        hp, kmean = _post_mixer(hp, conv_out, attn, p_prompt[l].reshape(batch * seq, -1), *post,
                                paged=(cache_k, l, pt_flat, ppb))
        kp_l.append(k.reshape(batch, seq, n_heads, hd))
        vp_l.append(v.reshape(batch, seq, n_heads, hd))
        cp_l.append(u.reshape(batch, seq, c)[:, seq - (w_dw.shape[1] - 1):, :])

        conv_out, new_state = _conv_sample(u_s, state_conv[l], *conv_w)
        h3 = lambda t: t.reshape(db, n_heads, hd)
        sel = _sample_topk(kmean.reshape(db, n_pages // ppb, n_heads, hd), h3(q_s))
        attn = _attn_sample(h3(q_s), h3(k_s), h3(v_s), cache_k, cache_v, l, pt_flat, sel.reshape(-1), n_pages, ppb)
        hs = _post_mixer(hs, conv_out.astype(BF16), attn.astype(BF16), p_sample[l].reshape(db, -1), *post)
        ks_l.append(k_s.reshape(db, 1, n_heads, hd))
        vs_l.append(v_s.reshape(db, 1, n_heads, hd))
        cs_l.append(new_state)

    y_prompt = hp.reshape(batch, seq, d)
    y_sample = hs.reshape(db, 1, d)
    stack = lambda xs: xs[0][None] if len(xs) == 1 else jnp.stack(xs)
    return (y_prompt, y_sample, stack(kp_l), stack(vp_l), stack(cp_l), stack(ks_l), stack(vs_l), stack(cs_l))
```

```python
import functools

import jax
import jax.numpy as jnp
from jax import lax
from jax.experimental import pallas as pl
from jax.experimental.pallas import tpu as pltpu

F32 = jnp.float32
BF16 = jnp.bfloat16

MOBA_BLOCK = 256
MOBA_TOPK = 3
ROPE_THETA = 10000.0
NORM_EPS = 1e-6
LOG2_E = 1.4426950408889634
LANES = 128
SUBLANES = 8
CONV_HALO = 32
MM_ROW_CHUNK = 512
NORM_ROW_CHUNK = 64
SAMPLE_HEAD_GROUP = 8
SAMPLE_GROUPS_BUFFERED = 2
VMEM_LIMIT = 60 * 1024 * 1024


def _params(*sem):
    return pltpu.CompilerParams(dimension_semantics=sem, vmem_limit_bytes=VMEM_LIMIT)


def _tile(n, pref):
    if n <= pref:
        return n
    t = pref
    while n % t:
        t //= 2
    return t


def _rmsnorm_kernel(x_ref, g_ref, o_ref):
    x = x_ref[...]
    ms = jnp.mean(x * x, axis=-1, keepdims=True)
    o_ref[...] = (x * lax.rsqrt(ms + NORM_EPS) * g_ref[...]).astype(o_ref.dtype)


def _rmsnorm(x, g, out_dtype):
    m, d = x.shape
    tm = _tile(m, 256)
    return pl.pallas_call(
        _rmsnorm_kernel,
        grid=(m // tm,),
        in_specs=[pl.BlockSpec((tm, d), lambda i: (i, 0)), pl.BlockSpec((1, d), lambda i: (0, 0))],
        out_specs=pl.BlockSpec((tm, d), lambda i: (i, 0)),
        out_shape=jax.ShapeDtypeStruct((m, d), out_dtype),
        compiler_params=_params("parallel"),
        name="rmsnorm",
    )(x, g.reshape(1, d))


def _mm_kernel(*refs, n_lhs, dots, n_extra, epilogue, side, n_after):
    n_dot = len(dots)
    groups = 2 if side else 1
    pos = 0
    lhs = [refs[pos + g * n_lhs:pos + (g + 1) * n_lhs] for g in range(groups)]
    pos += groups * n_lhs
    ws = refs[pos:pos + n_dot]
    pos += n_dot
    extras = [refs[pos + g * n_extra:pos + (g + 1) * n_extra] for g in range(groups)]
    pos += groups * n_extra + n_after
    outs = refs[pos:pos + groups]
    wbs = refs[pos + groups:pos + groups + n_dot]
    i = pl.program_id(1)

    @pl.when(i == 0)
    def _():
        for w, wb in zip(ws, wbs):
            wb[...] = w[...].astype(BF16)

    def run(g):
        rows = outs[g].shape[0]
        rc = MM_ROW_CHUNK if rows % MM_ROW_CHUNK == 0 else rows
        for r0 in range(0, rows, rc):
            accs = {}
            for (li, group), wb in zip(dots, wbs):
                a = lhs[g][li][r0:r0 + rc, :]
                if a.dtype != BF16:
                    a = a.astype(BF16)
                d = jnp.dot(a, wb[...], preferred_element_type=F32)
                accs[group] = d if group not in accs else accs[group] + d
            res = epilogue([accs[k] for k in sorted(accs)], [e[r0:r0 + rc, :] for e in extras[g]])
            outs[g][r0:r0 + rc, :] = res.astype(outs[g].dtype)

    run(0)
    if side:
        pl.when(i == pl.num_programs(1) - 1)(functools.partial(run, 1))


def _matmul(lhs, dots, extras, epilogue, n_cols, tn, out_dtype, name, side=None, tm_pref=1024, after=()):
    m = lhs[0].shape[0]
    tm = _tile(m, tm_pref)
    in_specs = [pl.BlockSpec((tm, a.shape[1]), lambda j, i: (i, 0)) for a in lhs]
    operands = list(lhs)
    if side:
        lhs_s, extras_s = side
        ms = lhs_s[0].shape[0]
        in_specs += [pl.BlockSpec((ms, a.shape[1]), lambda j, i: (0, 0)) for a in lhs_s]
        operands += list(lhs_s)
    scratch = []
    for li, w, layer, rb, co, _ in dots:
        kl = lhs[li].shape[1]
        assert co % tn == 0
        in_specs.append(pl.BlockSpec(
            (None, kl, tn), functools.partial(lambda j, i, layer, rb, cb: (layer, rb, cb + j), layer=layer, rb=rb,
                                              cb=co // tn)))
        operands.append(w)
        scratch.append(pltpu.VMEM((kl, tn), BF16))
    for arr, blk, imap in list(extras) + (list(extras_s) if side else []):
        in_specs.append(pl.BlockSpec(blk, imap))
        operands.append(arr)
    in_specs += [pl.BlockSpec(memory_space=pl.ANY)] * len(after)
    operands += list(after)
    out_specs = [pl.BlockSpec((tm, tn), lambda j, i: (i, j))]
    out_shape = [jax.ShapeDtypeStruct((m, n_cols), out_dtype)]
    if side:
        out_specs.append(pl.BlockSpec((ms, tn), lambda j, i: (0, j)))
        out_shape.append(jax.ShapeDtypeStruct((ms, n_cols), out_dtype))
    kern = functools.partial(_mm_kernel, n_lhs=len(lhs), dots=[(d[0], d[5]) for d in dots],
                             n_extra=len(extras), epilogue=epilogue, side=bool(side), n_after=len(after))
    outs = pl.pallas_call(
        kern,
        grid=(n_cols // tn, m // tm),
        in_specs=in_specs,
        out_specs=out_specs,
        out_shape=out_shape,
        scratch_shapes=scratch,
        compiler_params=_params("parallel", "arbitrary"),
        name=name,
    )(*operands)
    return outs if side else outs[0]


def _ep_plain(accs, extras):
    return accs[0]


def _ep_glu(accs, extras):
    return accs[0] * jax.nn.sigmoid(accs[1])


def _ep_resid(accs, extras):
    return extras[0] + accs[0]


def _ep_rope(accs, extras):
    cos, sin_signed = extras
    y = accs[0]
    hd = cos.shape[1]
    outs = []
    for h in range(y.shape[1] // hd):
        xh = y[:, h * hd:(h + 1) * hd]
        outs.append(xh * cos + pltpu.roll(xh, hd // 2, 1) * sin_signed)
    return jnp.concatenate(outs, axis=1)


def _rope_tables(pos0, s, hd):
    inv = 1.0 / (ROPE_THETA ** (jnp.arange(0, hd, 2, dtype=F32) / hd))
    pos = (jnp.arange(s, dtype=jnp.int32) + pos0).astype(F32)
    ang = pos[:, None] * inv[None, :]
    cos, sin = jnp.cos(ang), jnp.sin(ang)
    return jnp.concatenate([cos, cos], axis=-1), jnp.concatenate([-sin, sin], axis=-1)


def _rms_scale(x, g):
    ms = jnp.mean(x * x, axis=-1, keepdims=True)
    return x * lax.rsqrt(ms + NORM_EPS) * g


def _mlp_kernel(pt_ref, h_ref, g_ref, wu_ref, wd_ref, *refs, n_page, ppb):
    page_refs = refs[:n_page]
    o_ref = refs[n_page]
    z_ref = refs[-1]
    f = pl.program_id(1)

    @pl.when(f == 0)
    def _():
        rows = h_ref.shape[0]
        rc = _tile(rows, NORM_ROW_CHUNK)
        for r0 in range(0, rows, rc):
            h = h_ref[r0:r0 + rc, :]
            o_ref[r0:r0 + rc, :] = h
            z_ref[r0:r0 + rc, :] = _rms_scale(h, g_ref[...]).astype(BF16)

    hm = jnp.dot(z_ref[...], wu_ref[...], preferred_element_type=F32)
    hm = jnp.square(jnp.maximum(hm, 0.0)).astype(BF16)
    o_ref[...] += jnp.dot(hm, wd_ref[...], preferred_element_type=F32)

    if n_page:
        km_ref = refs[n_page + 1]
        ps = page_refs[0].shape[0]
        for r in range(n_page // ppb):
            s = jnp.sum(page_refs[r * ppb][...], axis=0)
            for e in range(1, ppb):
                s = s + jnp.sum(page_refs[r * ppb + e][...], axis=0)
            km_ref[r] = s * (1.0 / (ppb * ps))


def _mlp(h, g, w_up, w_down, paged=None):
    m, d = h.shape
    dff = w_up.shape[1]
    tm = _tile(m, 512)
    tf = _tile(dff, 512)
    nf = dff // tf
    steps = (m // tm) * nf
    in_specs = [
        pl.BlockSpec((tm, d), lambda i, f, pt: (i, 0), pipeline_mode=pl.Buffered(1)),
        pl.BlockSpec((1, d), lambda i, f, pt: (0, 0)),
        pl.BlockSpec((d, tf), lambda i, f, pt: (0, f)),
        pl.BlockSpec((tf, d), lambda i, f, pt: (f, 0)),
    ]
    out_specs = [pl.BlockSpec((tm, d), lambda i, f, pt: (i, 0))]
    out_shape = [jax.ShapeDtypeStruct((m, d), F32)]
    operands = [h, g.reshape(1, d), w_up, w_down]
    n_page, ppb = 0, 1
    if paged is None:
        pt_flat = jnp.zeros((1,), jnp.int32)
    else:
        cache_k, layer, pt_flat, ppb = paged
        _, _, ps, n_heads, hd = cache_k.shape
        n_slots = pt_flat.shape[0]
        assert n_slots % (steps * ppb) == 0
        n_page = n_slots // steps
        for e in range(n_page):
            in_specs.append(pl.BlockSpec(
                (None, None, ps, n_heads, hd),
                functools.partial(lambda i, f, pt, e: (layer, pt[(i * nf + f) * n_page + e], 0, 0, 0), e=e)))
        operands += [cache_k] * n_page
        out_specs.append(pl.BlockSpec((n_page // ppb, n_heads, hd), lambda i, f, pt: (i * nf + f, 0, 0)))
        out_shape.append(jax.ShapeDtypeStruct((n_slots // ppb, n_heads, hd), F32))
    outs = pl.pallas_call(
        functools.partial(_mlp_kernel, n_page=n_page, ppb=ppb),
        grid_spec=pltpu.PrefetchScalarGridSpec(
            num_scalar_prefetch=1, grid=(m // tm, nf), in_specs=in_specs, out_specs=out_specs,
            scratch_shapes=[pltpu.VMEM((tm, d), BF16)]),
        out_shape=out_shape,
        compiler_params=_params("parallel", "arbitrary"),
        name="mlp",
    )(pt_flat, *operands)
    return outs[0] if paged is None else outs


def _ple_kernel(h_ref, p_ref, g_ref, gf_ref, wg_ref, wp_ref, y_ref, z_ref, pb_ref, *, final_norm):
    j = pl.program_id(1)
    tn = wg_ref.shape[1]

    rows = h_ref.shape[0]
    rc = _tile(rows, NORM_ROW_CHUNK)

    @pl.when(j == 0)
    def _():
        for r0 in range(0, rows, rc):
            z_ref[r0:r0 + rc, :] = _rms_scale(h_ref[r0:r0 + rc, :], g_ref[...]).astype(BF16)
        pb_ref[...] = p_ref[...].astype(BF16)

    cols = pl.ds(pl.multiple_of(j * tn, tn), tn)
    gate = jnp.dot(z_ref[...], wg_ref[...], preferred_element_type=F32)
    emb = jnp.dot(pb_ref[...], wp_ref[...].astype(BF16), preferred_element_type=F32)
    y_ref[:, cols] = h_ref[:, cols] + jax.nn.sigmoid(gate) * emb

    if final_norm:
        @pl.when(j == pl.num_programs(1) - 1)
        def _():
            for r0 in range(0, rows, rc):
                y_ref[r0:r0 + rc, :] = _rms_scale(y_ref[r0:r0 + rc, :], gf_ref[...])


def _ple(h, p, g_ple, g_final, w_gate, w_proj, layer, final_norm):
    m, d = h.shape
    pdim = p.shape[1]
    tm = _tile(m, 512)
    tn = _tile(d, 1024)
    vec = pl.BlockSpec((1, d), lambda i, j: (0, 0))
    return pl.pallas_call(
        functools.partial(_ple_kernel, final_norm=final_norm),
        grid=(m // tm, d // tn),
        in_specs=[
            pl.BlockSpec((tm, d), lambda i, j: (i, 0), pipeline_mode=pl.Buffered(1)),
            pl.BlockSpec((tm, pdim), lambda i, j: (i, 0)),
            vec, vec,
            pl.BlockSpec((d, tn), lambda i, j: (0, j)),
            pl.BlockSpec((None, pdim, tn), lambda i, j: (layer, 0, j)),
        ],
        out_specs=pl.BlockSpec((tm, d), lambda i, j: (i, 0)),
        out_shape=jax.ShapeDtypeStruct((m, d), F32),
        scratch_shapes=[pltpu.VMEM((tm, d), BF16), pltpu.VMEM((tm, pdim), BF16)],
        compiler_params=_params("parallel", "arbitrary"),
        name="ple",
    )(h, p, g_ple.reshape(1, d), g_final.reshape(1, d), w_gate, w_proj)


def _cast_stream(weights, layer, steps, step_of):
    in_specs, out_specs, out_shapes = [], [], []
    for w in weights:
        _, r, c = w.shape
        rows = r // steps
        assert r % steps == 0 and rows % (2 * SUBLANES) == 0
        in_specs.append(pl.BlockSpec((None, rows, c), lambda *g: (layer, step_of(*g), 0)))
        out_specs.append(pl.BlockSpec((rows, c), lambda *g: (step_of(*g), 0)))
        out_shapes.append(jax.ShapeDtypeStruct((r, c), BF16))
    return in_specs, out_specs, out_shapes, list(weights)


def _cast_rows(src_refs, dst_refs):
    for src, dst in zip(src_refs, dst_refs):
        dst[...] = src[...].astype(BF16)


def _ln_swish(y, g, beta):
    mu = jnp.mean(y, axis=-1, keepdims=True)
    xc = y - mu
    var = jnp.mean(xc * xc, axis=-1, keepdims=True)
    yn = xc * lax.rsqrt(var + NORM_EPS) * g + beta
    return yn * jax.nn.sigmoid(yn)


def _conv_prompt_kernel(cur_ref, halo_ref, w_ref, b_ref, g_ref, beta_ref, *refs, kw, cc, n_cast):
    cast_src, refs = refs[:n_cast], refs[n_cast:]
    o_ref, cast_dst = refs[0], refs[1:1 + n_cast]
    ext_ref, sh_ref, y_ref = refs[1 + n_cast:]
    _cast_rows(cast_src, cast_dst)
    t = pl.program_id(1)
    tt, c = cur_ref.shape
    ext_ref[0:CONV_HALO, :] = jnp.where(t == 0, 0.0, halo_ref[...])
    ext_ref[CONV_HALO:CONV_HALO + tt, :] = cur_ref[...]
    off = CONV_HALO - (kw - 1)
    rows = sh_ref.shape[1]
    for phase in range(1, SUBLANES):
        sh_ref[phase - 1] = ext_ref[phase:phase + rows, :]
    groups = tt // SUBLANES

    def column_chunk(ci, carry):
        cols = pl.ds(pl.multiple_of(ci * cc, cc), cc)
        acc = None
        for k in range(kw):
            phase = (off + k) % SUBLANES
            a0 = off + k - phase
            src = ext_ref if phase == 0 else sh_ref.at[phase - 1]
            term = src[a0:a0 + tt, cols].reshape(groups, SUBLANES, cc) * w_ref[k, :, cols][None]
            acc = term if acc is None else acc + term
        y_ref[:, cols] = acc.reshape(tt, cc) + b_ref[:, cols]
        return carry

    lax.fori_loop(0, c // cc, column_chunk, 0)
    rc = 2 * SUBLANES
    for r0 in range(0, tt, rc):
        o_ref[r0:r0 + rc, :] = _ln_swish(y_ref[r0:r0 + rc, :], g_ref[...], beta_ref[...]).astype(o_ref.dtype)


def _conv_prompt(u, batch, seq, w_dw, b_dw, ln_g, ln_b, cast=(), layer=0):
    m, c = u.shape
    kw = w_dw.shape[0]
    assert kw - 1 <= CONV_HALO
    tt = _tile(seq, 128)
    assert tt % CONV_HALO == 0
    nt = seq // tt
    hb = tt // CONV_HALO
    vec = lambda a: a.reshape(1, c)
    vspec = pl.BlockSpec((1, c), lambda b, t: (0, 0))
    kern = functools.partial(_conv_prompt_kernel, kw=kw, cc=min(LANES, c), n_cast=len(cast))
    w_rep = jnp.broadcast_to(w_dw[:, None, :], (kw, SUBLANES, c))
    c_in, c_out, c_shape, c_ops = _cast_stream(cast, layer, batch * nt, lambda b, t: b * nt + t)
    outs = pl.pallas_call(
        kern,
        grid=(batch, nt),
        in_specs=[
            pl.BlockSpec((tt, c), lambda b, t: (b * nt + t, 0)),
            pl.BlockSpec((CONV_HALO, c), lambda b, t: (jnp.maximum((b * nt + t) * hb - 1, 0), 0)),
            pl.BlockSpec((kw, SUBLANES, c), lambda b, t: (0, 0, 0)),
            vspec, vspec, vspec,
        ] + c_in,
        out_specs=[pl.BlockSpec((tt, c), lambda b, t: (b * nt + t, 0))] + c_out,
        out_shape=[jax.ShapeDtypeStruct((m, c), BF16)] + c_shape,
        scratch_shapes=[pltpu.VMEM((CONV_HALO + tt, c), F32),
                        pltpu.VMEM((SUBLANES - 1, CONV_HALO + tt - SUBLANES, c), F32),
                        pltpu.VMEM((tt, c), F32)],
        compiler_params=_params("parallel", "parallel"),
        name="conv_prompt",
    )(u, u, w_rep, vec(b_dw), vec(ln_g), vec(ln_b), *c_ops)
    return outs[0] if not cast else outs


def _conv_sample_kernel(st_ref, u_ref, w_ref, b_ref, g_ref, beta_ref, o_ref, ns_ref):
    ks = st_ref.shape[0]
    st = st_ref[...]
    u = u_ref[...]
    y = jnp.sum(st * w_ref[0:ks, :], axis=0, keepdims=True) + u * w_ref[ks:ks + 1, :] + b_ref[...]
    o_ref[...] = _ln_swish(y, g_ref[...], beta_ref[...]).astype(o_ref.dtype)
    ns_ref[0:ks - 1, :] = st_ref[1:ks, :]
    ns_ref[ks - 1:ks, :] = u


def _conv_sample(u, state, w_dw, b_dw, ln_g, ln_b):
    db, c = u.shape
    ks = state.shape[1]
    vec = lambda a: a.reshape(1, c)
    vspec = pl.BlockSpec((1, c), lambda b: (0, 0))
    out, new_state = pl.pallas_call(
        _conv_sample_kernel,
        grid=(db,),
        in_specs=[
            pl.BlockSpec((None, ks, c), lambda b: (b, 0, 0)),
            pl.BlockSpec((None, 1, c), lambda b: (b, 0, 0)),
            pl.BlockSpec((ks + 1, c), lambda b: (0, 0)),
            vspec, vspec, vspec,
        ],
        out_specs=[pl.BlockSpec((None, 1, c), lambda b: (b, 0, 0)), pl.BlockSpec((None, ks, c), lambda b: (b, 0, 0))],
        out_shape=[jax.ShapeDtypeStruct((db, 1, c), F32), jax.ShapeDtypeStruct((db, ks, c), F32)],
        compiler_params=_params("parallel"),
        name="conv_sample",
    )(state, u.reshape(db, 1, c), w_dw, vec(b_dw), vec(ln_g), vec(ln_b))
    return out.reshape(db, c), new_state


def _attn_block_body(jj, q_ref, kb_ref, vt_ref, km_ref, o_ref, scale):
    blk = MOBA_BLOCK
    q = q_ref[jj * blk:(jj + 1) * blk, :]
    qb = q.astype(BF16)
    qs = (q * (scale * LOG2_E)).astype(BF16)
    ln = (jj + 1) * blk
    nt = (((1,), (1,)), ((), ()))
    st = lax.dot_general(kb_ref[0:ln, :], qs, nt, preferred_element_type=F32)
    pieces = []
    if jj > MOBA_TOPK:
        g = lax.dot_general(km_ref[...].astype(BF16), qb, nt, preferred_element_type=F32)
        for n in range(jj):
            cnt = jnp.zeros((1, blk), F32)
            for m in range(jj):
                if m == n:
                    continue
                beats = (g[m:m + 1] >= g[n:n + 1]) if m < n else (g[m:m + 1] > g[n:n + 1])
                cnt = cnt + jnp.where(beats, 1.0, 0.0)
            pieces.append(jnp.where(cnt < MOBA_TOPK, st[n * blk:(n + 1) * blk, :], -jnp.inf))
    else:
        pieces = [st[n * blk:(n + 1) * blk, :] for n in range(jj)]
    key = lax.broadcasted_iota(jnp.int32, (blk, blk), 0)
    qry = lax.broadcasted_iota(jnp.int32, (blk, blk), 1)
    pieces.append(jnp.where(key <= qry, st[jj * blk:ln, :], -jnp.inf))
    mx = jnp.max(pieces[-1], axis=0, keepdims=True)
    for pc in pieces[:-1]:
        mx = jnp.maximum(mx, jnp.max(pc, axis=0, keepdims=True))
    ps = [jnp.exp2(pc - mx) for pc in pieces]
    den = ps[0].sum(axis=0, keepdims=True)
    for p in ps[1:]:
        den = den + p.sum(axis=0, keepdims=True)
    pcat = jnp.concatenate(ps, axis=0) if len(ps) > 1 else ps[0]
    acc_t = jnp.dot(vt_ref[:, 0:ln], pcat.astype(BF16), preferred_element_type=F32)
    o_ref[jj * blk:(jj + 1) * blk, :] = (acc_t / den).T.astype(o_ref.dtype)


def _attn_prompt_kernel(q_ref, k_ref, v_ref, *refs, nb, scale, n_cast):
    cast_src, refs = refs[:n_cast], refs[n_cast:]
    o_ref, cast_dst = refs[0], refs[1:1 + n_cast]
    kb_ref, vt_ref, km_ref = refs[1 + n_cast:]
    _cast_rows(cast_src, cast_dst)
    blk = MOBA_BLOCK
    kb_ref[...] = k_ref[...].astype(BF16)
    km_ref[...] = jnp.zeros(km_ref.shape, F32)
    for n in range(nb):
        vt_ref[:, n * blk:(n + 1) * blk] = v_ref[n * blk:(n + 1) * blk, :].T.astype(BF16)
        km_ref[n:n + 1, :] = jnp.mean(k_ref[n * blk:(n + 1) * blk, :], axis=0, keepdims=True)
    for jj in range(nb):
        _attn_block_body(jj, q_ref, kb_ref, vt_ref, km_ref, o_ref, scale)


def _attn_prompt(q, k, v, batch, seq, n_heads, hd, cast=(), layer=0):
    assert seq % MOBA_BLOCK == 0
    nb = seq // MOBA_BLOCK
    m = batch * seq
    bf16_rows = 2 * SUBLANES
    km_rows = -(-nb // bf16_rows) * bf16_rows
    kern = functools.partial(_attn_prompt_kernel, nb=nb, scale=hd ** -0.5, n_cast=len(cast))
    spec = pl.BlockSpec((seq, hd), lambda b, h: (b, h))
    c_in, c_out, c_shape, c_ops = _cast_stream(cast, layer, batch * n_heads, lambda b, h: b * n_heads + h)
    outs = pl.pallas_call(
        kern,
        grid=(batch, n_heads),
        in_specs=[spec, spec, spec] + c_in,
        out_specs=[spec] + c_out,
        out_shape=[jax.ShapeDtypeStruct((m, n_heads * hd), BF16)] + c_shape,
        scratch_shapes=[pltpu.VMEM((seq, hd), BF16), pltpu.VMEM((hd, seq), BF16), pltpu.VMEM((km_rows, hd), F32)],
        compiler_params=_params("parallel", "parallel"),
        name="attn_prompt",
    )(q, k, v, *c_ops)
    return outs[0] if not cast else outs


def _topk_kernel(km_ref, q_ref, o_ref):
    nblk = km_ref.shape[0]
    gate = jnp.sum(km_ref[...] * q_ref[...][None], axis=-1, keepdims=True)
    blk_id = lax.broadcasted_iota(jnp.int32, gate.shape, 0)
    for s in range(MOBA_TOPK):
        mx = jnp.max(gate, axis=0, keepdims=True)
        idx = jnp.min(jnp.where(gate == mx, blk_id, nblk), axis=0, keepdims=True)
        o_ref[s:s + 1] = idx
        gate = jnp.where(blk_id == idx, -jnp.inf, gate)


def _sample_topk(kmean, q):
    db, nblk, n_heads, hd = kmean.shape
    assert nblk >= MOBA_TOPK
    out = pl.pallas_call(
        _topk_kernel,
        grid=(db,),
        in_specs=[pl.BlockSpec((None, nblk, n_heads, hd), lambda b: (b, 0, 0, 0)),
                  pl.BlockSpec((None, n_heads, hd), lambda b: (b, 0, 0))],
        out_specs=pl.BlockSpec((None, MOBA_TOPK, n_heads, 1), lambda b: (b, 0, 0, 0)),
        out_shape=jax.ShapeDtypeStruct((db, MOBA_TOPK, n_heads, 1), jnp.int32),
        compiler_params=_params("parallel"),
        name="sample_topk",
    )(kmean, q)
    return jnp.swapaxes(out.reshape(db, MOBA_TOPK, n_heads), 1, 2)


def _attn_sample_kernel(pt_ref, sel_ref, q_ref, kn_ref, vn_ref, ck_hbm, cv_hbm, o_ref, kbuf, vbuf, sems, *,
                        layer, n_pages, ppb, scale):
    b = pl.program_id(0)
    n_heads, hd = q_ref.shape
    n_slab = MOBA_TOPK * ppb

    def slab_copies(h, slot):
        cps = []
        for s in range(MOBA_TOPK):
            blk = sel_ref[(b * n_heads + h) * MOBA_TOPK + s]
            for e in range(ppb):
                page = pt_ref[b * n_pages + blk * ppb + e]
                i = s * ppb + e
                cps.append(pltpu.make_async_copy(ck_hbm.at[layer, page, :, h, :], kbuf.at[slot, i], sems.at[0, slot, i]))
                cps.append(pltpu.make_async_copy(cv_hbm.at[layer, page, :, h, :], vbuf.at[slot, i], sems.at[1, slot, i]))
        return cps

    n_slot = kbuf.shape[0]
    group = n_slot // SAMPLE_GROUPS_BUFFERED
    n_key = n_slab * kbuf.shape[2]

    def group_copies(g):
        return [cp for h in range(g * group, (g + 1) * group) for cp in slab_copies(h, h % n_slot)]

    n_group = n_heads // group
    ahead = n_slot // group - 1
    for g0 in range(min(ahead, n_group)):
        for cp in group_copies(g0):
            cp.start()
    for h in range(n_heads):
        slot = h % n_slot
        if h % group == 0:
            g = h // group
            if g + ahead < n_group:
                for cp in group_copies(g + ahead):
                    cp.start()
            for cp in group_copies(g):
                cp.wait()
        q = q_ref[h:h + 1, :]
        qm = jnp.broadcast_to(q, (LANES, hd)).astype(BF16)
        keys = kbuf[slot].reshape(n_key, hd).astype(BF16)
        s = lax.dot_general(keys, qm, (((1,), (1,)), ((), ())), preferred_element_type=F32) * scale
        s_self = jnp.sum(q * kn_ref[h:h + 1, :], axis=1, keepdims=True) * scale
        mx = jnp.maximum(jnp.max(s, axis=0, keepdims=True), s_self)
        p = jnp.exp(s - mx)
        p_self = jnp.exp(s_self - mx)
        den = jnp.sum(p, axis=0, keepdims=True) + p_self
        acc = jnp.sum(p * vbuf[slot].reshape(n_key, hd), axis=0, keepdims=True) + p_self * vn_ref[h:h + 1, :]
        o_ref[h:h + 1, :] = acc / den


def _attn_sample(q, k_new, v_new, cache_k, cache_v, layer, pt_flat, sel_flat, n_pages, ppb):
    db, n_heads, hd = q.shape
    ps = cache_k.shape[2]
    n_slab = MOBA_TOPK * ppb
    n_slot = _tile(n_heads, SAMPLE_HEAD_GROUP) * SAMPLE_GROUPS_BUFFERED
    assert hd == LANES
    row_spec = pl.BlockSpec((None, n_heads, hd), lambda b, pt, sel: (b, 0, 0))
    any_spec = pl.BlockSpec(memory_space=pl.ANY)
    out = pl.pallas_call(
        functools.partial(_attn_sample_kernel, layer=layer, n_pages=n_pages, ppb=ppb, scale=hd ** -0.5),
        grid_spec=pltpu.PrefetchScalarGridSpec(
            num_scalar_prefetch=2,
            grid=(db,),
            in_specs=[row_spec, row_spec, row_spec, any_spec, any_spec],
            out_specs=row_spec,
            scratch_shapes=[pltpu.VMEM((n_slot, n_slab, ps, hd), F32), pltpu.VMEM((n_slot, n_slab, ps, hd), F32),
                            pltpu.SemaphoreType.DMA((2, n_slot, n_slab))],
        ),
        out_shape=jax.ShapeDtypeStruct((db, n_heads, hd), F32),
        compiler_params=_params("arbitrary"),
        name="attn_sample",
    )(pt_flat, sel_flat, q, k_new, v_new, cache_k, cache_v)
    return out.reshape(db, n_heads * hd)


def _in_projection(z, z_s, w_in, layer, c, a, rope_p, rope_s, table_rows, hd):
    m, ms = z.shape[0], z_s.shape[0]
    tm = _tile(m, 1024)
    tn_glu = _tile(c, 256)
    tn = _tile(a, 512)
    assert table_rows % tm == 0
    period = table_rows // tm
    tbl = [(t, (tm, hd), lambda j, i: (i % period, 0)) for t in rope_p]
    tbl_s = [(t, (ms, hd), lambda j, i: (0, 0)) for t in rope_s]
    col = lambda off, group=0: (0, w_in, layer, 0, off, group)
    u = _matmul([z], [col(0), col(c, 1)], [], _ep_glu, c, tn_glu, F32, "in_proj_glu", side=([z_s], []))
    q = _matmul([z], [col(2 * c)], tbl, _ep_rope, a, tn, F32, "in_proj_q", side=([z_s], tbl_s), after=(u[0],))
    k = _matmul([z], [col(2 * c + a)], tbl, _ep_rope, a, tn, F32, "in_proj_k", side=([z_s], tbl_s), after=(q[0],))
    v = _matmul([z], [col(2 * c + 2 * a)], [], _ep_plain, a, tn, F32, "in_proj_v", side=([z_s], []), after=(k[0],))
    return tuple(zip(u, q, k, v))


def _post_mixer(x, conv_out, attn, p, layer, w_out, g_ffn, w_up, w_down, g_ple, w_ple_gate, w_ple_proj, g_final,
                final_norm, paged=None):
    m, d = x.shape
    c = conv_out.shape[1]
    assert attn.shape[1] == c
    tm = _tile(m, 1024)
    tn = _tile(d, 512)
    resid = lambda r: (r, (tm, tn), lambda j, i: (i, j))
    h = _matmul([conv_out, attn], [(0, w_out, layer, 0, 0, 0), (1, w_out, layer, 1, 0, 0)], [resid(x)], _ep_resid,
                d, tn, F32, "out_proj")
    h = _mlp(h, g_ffn, w_up, w_down, paged)
    if paged is not None:
        h, kmean = h
    out = _ple(h, p, g_ple, g_final, w_ple_gate, w_ple_proj, layer, final_norm)
    return out if paged is None else (out, kmean)


def kernel(x_prompt, x_sample, cache_k, cache_v, state_conv, page_table, p_prompt, p_sample, g_mix, w_in, w_dw, b_dw, g_conv_ln, b_conv_ln, w_out, g_ffn, w_up, w_down, g_ple, w_ple_gate, w_ple_proj, g_final):
    batch, seq, d = x_prompt.shape
    db, dseq, _ = x_sample.shape
    assert dseq == 1
    depth, n_pool, page, n_heads, hd = cache_k.shape
    n_pages = page_table.shape[1]
    past_len = n_pages * page
    assert MOBA_BLOCK % page == 0 and past_len % MOBA_BLOCK == 0
    ppb = MOBA_BLOCK // page
    c = w_dw.shape[-1]
    a = n_heads * hd
    assert c + a == d

    hp = x_prompt.reshape(batch * seq, d)
    hs = x_sample.reshape(db, d)
    cos_p, sin_p = _rope_tables(0, seq, hd)
    cos_s, sin_s = _rope_tables(past_len, 1, hd)
    cos_s = jnp.broadcast_to(cos_s, (db, hd))
    sin_s = jnp.broadcast_to(sin_s, (db, hd))
    pt_flat = page_table.reshape(-1).astype(jnp.int32)

    kp_l, vp_l, cp_l, ks_l, vs_l, cs_l = [], [], [], [], [], []
    for l in range(depth):
        conv_w = (w_dw[l], b_dw[l], g_conv_ln[l], b_conv_ln[l])

        z = _rmsnorm(hp, g_mix[l], BF16)
        z_s = _rmsnorm(hs, g_mix[l], BF16)
        (u, q, k, v), (u_s, q_s, k_s, v_s) = _in_projection(z, z_s, w_in, l, c, a, (cos_p, sin_p), (cos_s, sin_s),
                                                            seq, hd)

        conv_out, w_down_b, w_pg_b = _conv_prompt(u, batch, seq, *conv_w, cast=(w_down, w_ple_gate), layer=l)
        attn, w_up_b = _attn_prompt(q, k, v, batch, seq, n_heads, hd, cast=(w_up,), layer=l)
        post = (l, w_out, g_ffn[l], w_up_b, w_down_b, g_ple[l], w_pg_b, w_ple_proj, g_final, l == depth - 1)
        hp, kmean = _post_mixer(hp, conv_out, attn, p_prompt[l].reshape(batch * seq, -1), *post,
                                paged=(cache_k, l, pt_flat, ppb))
        kp_l.append(k.reshape(batch, seq, n_heads, hd))
        vp_l.append(v.reshape(batch, seq, n_heads, hd))
        cp_l.append(u.reshape(batch, seq, c)[:, seq - (w_dw.shape[1] - 1):, :])

        conv_out, new_state = _conv_sample(u_s, state_conv[l], *conv_w)
        h3 = lambda t: t.reshape(db, n_heads, hd)
        sel = _sample_topk(kmean.reshape(db, n_pages // ppb, n_heads, hd), h3(q_s))
        attn = _attn_sample(h3(q_s), h3(k_s), h3(v_s), cache_k, cache_v, l, pt_flat, sel.reshape(-1), n_pages, ppb)
        hs = _post_mixer(hs, conv_out.astype(BF16), attn.astype(BF16), p_sample[l].reshape(db, -1), *post)
        ks_l.append(k_s.reshape(db, 1, n_heads, hd))
        vs_l.append(v_s.reshape(db, 1, n_heads, hd))
        cs_l.append(new_state)

    y_prompt = hp.reshape(batch, seq, d)
    y_sample = hs.reshape(db, 1, d)
    stack = lambda xs: xs[0][None] if len(xs) == 1 else jnp.stack(xs)
    return (y_prompt, y_sample, stack(kp_l), stack(vp_l), stack(cp_l), stack(ks_l), stack(vs_l), stack(cs_l))
```

```python
import functools

import jax
import jax.numpy as jnp
from jax import lax
from jax.experimental import pallas as pl
from jax.experimental.pallas import tpu as pltpu

F32 = jnp.float32
BF16 = jnp.bfloat16

MOBA_BLOCK = 256
MOBA_TOPK = 3
ROPE_THETA = 10000.0
NORM_EPS = 1e-6
LOG2_E = 1.4426950408889634
LANES = 128
SUBLANES = 8
CONV_HALO = 32
MM_ROW_CHUNK = 512
NORM_ROW_CHUNK = 64
SAMPLE_HEAD_GROUP = 4
SAMPLE_GROUPS_BUFFERED = 3
VMEM_LIMIT = 60 * 1024 * 1024


def _params(*sem):
    return pltpu.CompilerParams(dimension_semantics=sem, vmem_limit_bytes=VMEM_LIMIT)


def _tile(n, pref):
    if n <= pref:
        return n
    t = pref
    while n % t:
        t //= 2
    return t


def _rmsnorm_kernel(x_ref, g_ref, o_ref):
    x = x_ref[...]
    ms = jnp.mean(x * x, axis=-1, keepdims=True)
    o_ref[...] = (x * lax.rsqrt(ms + NORM_EPS) * g_ref[...]).astype(o_ref.dtype)


def _rmsnorm(x, g, out_dtype):
    m, d = x.shape
    tm = _tile(m, 256)
    return pl.pallas_call(
        _rmsnorm_kernel,
        grid=(m // tm,),
        in_specs=[pl.BlockSpec((tm, d), lambda i: (i, 0)), pl.BlockSpec((1, d), lambda i: (0, 0))],
        out_specs=pl.BlockSpec((tm, d), lambda i: (i, 0)),
        out_shape=jax.ShapeDtypeStruct((m, d), out_dtype),
        compiler_params=_params("parallel"),
        name="rmsnorm",
    )(x, g.reshape(1, d))


def _mm_kernel(*refs, n_lhs, dots, n_extra, epilogue, side, n_after):
    n_dot = len(dots)
    groups = 2 if side else 1
    pos = 0
    lhs = [refs[pos + g * n_lhs:pos + (g + 1) * n_lhs] for g in range(groups)]
    pos += groups * n_lhs
    ws = refs[pos:pos + n_dot]
    pos += n_dot
    extras = [refs[pos + g * n_extra:pos + (g + 1) * n_extra] for g in range(groups)]
    pos += groups * n_extra + n_after
    outs = refs[pos:pos + groups]
    wbs = refs[pos + groups:pos + groups + n_dot]
    i = pl.program_id(1)

    @pl.when(i == 0)
    def _():
        for w, wb in zip(ws, wbs):
            wb[...] = w[...].astype(BF16)

    def run(g):
        rows = outs[g].shape[0]
        rc = MM_ROW_CHUNK if rows % MM_ROW_CHUNK == 0 else rows
        for r0 in range(0, rows, rc):
            accs = {}
            for (li, group), wb in zip(dots, wbs):
                a = lhs[g][li][r0:r0 + rc, :]
                if a.dtype != BF16:
                    a = a.astype(BF16)
                d = jnp.dot(a, wb[...], preferred_element_type=F32)
                accs[group] = d if group not in accs else accs[group] + d
            res = epilogue([accs[k] for k in sorted(accs)], [e[r0:r0 + rc, :] for e in extras[g]])
            outs[g][r0:r0 + rc, :] = res.astype(outs[g].dtype)

    run(0)
    if side:
        pl.when(i == pl.num_programs(1) - 1)(functools.partial(run, 1))


def _matmul(lhs, dots, extras, epilogue, n_cols, tn, out_dtype, name, side=None, tm_pref=1024, after=()):
    m = lhs[0].shape[0]
    tm = _tile(m, tm_pref)
    in_specs = [pl.BlockSpec((tm, a.shape[1]), lambda j, i: (i, 0)) for a in lhs]
    operands = list(lhs)
    if side:
        lhs_s, extras_s = side
        ms = lhs_s[0].shape[0]
        in_specs += [pl.BlockSpec((ms, a.shape[1]), lambda j, i: (0, 0)) for a in lhs_s]
        operands += list(lhs_s)
    scratch = []
    for li, w, layer, rb, co, _ in dots:
        kl = lhs[li].shape[1]
        assert co % tn == 0
        in_specs.append(pl.BlockSpec(
            (None, kl, tn), functools.partial(lambda j, i, layer, rb, cb: (layer, rb, cb + j), layer=layer, rb=rb,
                                              cb=co // tn)))
        operands.append(w)
        scratch.append(pltpu.VMEM((kl, tn), BF16))
    for arr, blk, imap in list(extras) + (list(extras_s) if side else []):
        in_specs.append(pl.BlockSpec(blk, imap))
        operands.append(arr)
    in_specs += [pl.BlockSpec(memory_space=pl.ANY)] * len(after)
    operands += list(after)
    out_specs = [pl.BlockSpec((tm, tn), lambda j, i: (i, j))]
    out_shape = [jax.ShapeDtypeStruct((m, n_cols), out_dtype)]
    if side:
        out_specs.append(pl.BlockSpec((ms, tn), lambda j, i: (0, j)))
        out_shape.append(jax.ShapeDtypeStruct((ms, n_cols), out_dtype))
    kern = functools.partial(_mm_kernel, n_lhs=len(lhs), dots=[(d[0], d[5]) for d in dots],
                             n_extra=len(extras), epilogue=epilogue, side=bool(side), n_after=len(after))
    outs = pl.pallas_call(
        kern,
        grid=(n_cols // tn, m // tm),
        in_specs=in_specs,
        out_specs=out_specs,
        out_shape=out_shape,
        scratch_shapes=scratch,
        compiler_params=_params("parallel", "arbitrary"),
        name=name,
    )(*operands)
    return outs if side else outs[0]


def _ep_plain(accs, extras):
    return accs[0]


def _ep_glu(accs, extras):
    return accs[0] * jax.nn.sigmoid(accs[1])


def _ep_resid(accs, extras):
    return extras[0] + accs[0]


def _ep_rope(accs, extras):
    cos, sin_signed = extras
    y = accs[0]
    hd = cos.shape[1]
    outs = []
    for h in range(y.shape[1] // hd):
        xh = y[:, h * hd:(h + 1) * hd]
        outs.append(xh * cos + pltpu.roll(xh, hd // 2, 1) * sin_signed)
    return jnp.concatenate(outs, axis=1)


def _rope_tables(pos0, s, hd):
    inv = 1.0 / (ROPE_THETA ** (jnp.arange(0, hd, 2, dtype=F32) / hd))
    pos = (jnp.arange(s, dtype=jnp.int32) + pos0).astype(F32)
    ang = pos[:, None] * inv[None, :]
    cos, sin = jnp.cos(ang), jnp.sin(ang)
    return jnp.concatenate([cos, cos], axis=-1), jnp.concatenate([-sin, sin], axis=-1)


def _rms_scale(x, g):
    ms = jnp.mean(x * x, axis=-1, keepdims=True)
    return x * lax.rsqrt(ms + NORM_EPS) * g


def _mlp_kernel(pt_ref, h_ref, g_ref, wu_ref, wd_ref, *refs, n_page, ppb):
    page_refs = refs[:n_page]
    o_ref = refs[n_page]
    z_ref = refs[-1]
    f = pl.program_id(1)

    @pl.when(f == 0)
    def _():
        rows = h_ref.shape[0]
        rc = _tile(rows, NORM_ROW_CHUNK)
        for r0 in range(0, rows, rc):
            h = h_ref[r0:r0 + rc, :]
            o_ref[r0:r0 + rc, :] = h
            z_ref[r0:r0 + rc, :] = _rms_scale(h, g_ref[...]).astype(BF16)

    hm = jnp.dot(z_ref[...], wu_ref[...], preferred_element_type=F32)
    hm = jnp.square(jnp.maximum(hm, 0.0)).astype(BF16)
    o_ref[...] += jnp.dot(hm, wd_ref[...], preferred_element_type=F32)

    if n_page:
        km_ref = refs[n_page + 1]
        ps = page_refs[0].shape[0]
        for r in range(n_page // ppb):
            s = jnp.sum(page_refs[r * ppb][...], axis=0)
            for e in range(1, ppb):
                s = s + jnp.sum(page_refs[r * ppb + e][...], axis=0)
            km_ref[r] = s * (1.0 / (ppb * ps))


def _mlp(h, g, w_up, w_down, paged=None):
    m, d = h.shape
    dff = w_up.shape[1]
    tm = _tile(m, 512)
    tf = _tile(dff, 512)
    nf = dff // tf
    steps = (m // tm) * nf
    in_specs = [
        pl.BlockSpec((tm, d), lambda i, f, pt: (i, 0), pipeline_mode=pl.Buffered(1)),
        pl.BlockSpec((1, d), lambda i, f, pt: (0, 0)),
        pl.BlockSpec((d, tf), lambda i, f, pt: (0, f)),
        pl.BlockSpec((tf, d), lambda i, f, pt: (f, 0)),
    ]
    out_specs = [pl.BlockSpec((tm, d), lambda i, f, pt: (i, 0))]
    out_shape = [jax.ShapeDtypeStruct((m, d), F32)]
    operands = [h, g.reshape(1, d), w_up, w_down]
    n_page, ppb = 0, 1
    if paged is None:
        pt_flat = jnp.zeros((1,), jnp.int32)
    else:
        cache_k, layer, pt_flat, ppb = paged
        _, _, ps, n_heads, hd = cache_k.shape
        n_slots = pt_flat.shape[0]
        assert n_slots % (steps * ppb) == 0
        n_page = n_slots // steps
        for e in range(n_page):
            in_specs.append(pl.BlockSpec(
                (None, None, ps, n_heads, hd),
                functools.partial(lambda i, f, pt, e: (layer, pt[(i * nf + f) * n_page + e], 0, 0, 0), e=e)))
        operands += [cache_k] * n_page
        out_specs.append(pl.BlockSpec((n_page // ppb, n_heads, hd), lambda i, f, pt: (i * nf + f, 0, 0)))
        out_shape.append(jax.ShapeDtypeStruct((n_slots // ppb, n_heads, hd), F32))
    outs = pl.pallas_call(
        functools.partial(_mlp_kernel, n_page=n_page, ppb=ppb),
        grid_spec=pltpu.PrefetchScalarGridSpec(
            num_scalar_prefetch=1, grid=(m // tm, nf), in_specs=in_specs, out_specs=out_specs,
            scratch_shapes=[pltpu.VMEM((tm, d), BF16)]),
        out_shape=out_shape,
        compiler_params=_params("parallel", "arbitrary"),
        name="mlp",
    )(pt_flat, *operands)
    return outs[0] if paged is None else outs


def _ple_kernel(h_ref, p_ref, g_ref, gf_ref, wg_ref, wp_ref, y_ref, z_ref, pb_ref, *, final_norm):
    j = pl.program_id(1)
    tn = wg_ref.shape[1]

    rows = h_ref.shape[0]
    rc = _tile(rows, NORM_ROW_CHUNK)

    @pl.when(j == 0)
    def _():
        for r0 in range(0, rows, rc):
            z_ref[r0:r0 + rc, :] = _rms_scale(h_ref[r0:r0 + rc, :], g_ref[...]).astype(BF16)
        pb_ref[...] = p_ref[...].astype(BF16)

    cols = pl.ds(pl.multiple_of(j * tn, tn), tn)
    gate = jnp.dot(z_ref[...], wg_ref[...], preferred_element_type=F32)
    emb = jnp.dot(pb_ref[...], wp_ref[...].astype(BF16), preferred_element_type=F32)
    y_ref[:, cols] = h_ref[:, cols] + jax.nn.sigmoid(gate) * emb

    if final_norm:
        @pl.when(j == pl.num_programs(1) - 1)
        def _():
            for r0 in range(0, rows, rc):
                y_ref[r0:r0 + rc, :] = _rms_scale(y_ref[r0:r0 + rc, :], gf_ref[...])


def _ple(h, p, g_ple, g_final, w_gate, w_proj, layer, final_norm):
    m, d = h.shape
    pdim = p.shape[1]
    tm = _tile(m, 512)
    tn = _tile(d, 1024)
    vec = pl.BlockSpec((1, d), lambda i, j: (0, 0))
    return pl.pallas_call(
        functools.partial(_ple_kernel, final_norm=final_norm),
        grid=(m // tm, d // tn),
        in_specs=[
            pl.BlockSpec((tm, d), lambda i, j: (i, 0), pipeline_mode=pl.Buffered(1)),
            pl.BlockSpec((tm, pdim), lambda i, j: (i, 0)),
            vec, vec,
            pl.BlockSpec((d, tn), lambda i, j: (0, j)),
            pl.BlockSpec((None, pdim, tn), lambda i, j: (layer, 0, j)),
        ],
        out_specs=pl.BlockSpec((tm, d), lambda i, j: (i, 0)),
        out_shape=jax.ShapeDtypeStruct((m, d), F32),
        scratch_shapes=[pltpu.VMEM((tm, d), BF16), pltpu.VMEM((tm, pdim), BF16)],
        compiler_params=_params("parallel", "arbitrary"),
        name="ple",
    )(h, p, g_ple.reshape(1, d), g_final.reshape(1, d), w_gate, w_proj)


def _cast_stream(weights, layer, steps, step_of):
    in_specs, out_specs, out_shapes = [], [], []
    for w in weights:
        _, r, c = w.shape
        rows = r // steps
        assert r % steps == 0 and rows % (2 * SUBLANES) == 0
        in_specs.append(pl.BlockSpec((None, rows, c), lambda *g: (layer, step_of(*g), 0)))
        out_specs.append(pl.BlockSpec((rows, c), lambda *g: (step_of(*g), 0)))
        out_shapes.append(jax.ShapeDtypeStruct((r, c), BF16))
    return in_specs, out_specs, out_shapes, list(weights)


def _cast_rows(src_refs, dst_refs):
    for src, dst in zip(src_refs, dst_refs):
        dst[...] = src[...].astype(BF16)


def _ln_swish(y, g, beta):
    mu = jnp.mean(y, axis=-1, keepdims=True)
    xc = y - mu
    var = jnp.mean(xc * xc, axis=-1, keepdims=True)
    yn = xc * lax.rsqrt(var + NORM_EPS) * g + beta
    return yn * jax.nn.sigmoid(yn)


def _conv_prompt_kernel(cur_ref, halo_ref, w_ref, b_ref, g_ref, beta_ref, *refs, kw, cc, n_cast):
    cast_src, refs = refs[:n_cast], refs[n_cast:]
    o_ref, cast_dst = refs[0], refs[1:1 + n_cast]
    ext_ref, sh_ref, y_ref = refs[1 + n_cast:]
    _cast_rows(cast_src, cast_dst)
    t = pl.program_id(1)
    tt, c = cur_ref.shape
    ext_ref[0:CONV_HALO, :] = jnp.where(t == 0, 0.0, halo_ref[...])
    ext_ref[CONV_HALO:CONV_HALO + tt, :] = cur_ref[...]
    off = CONV_HALO - (kw - 1)
    rows = sh_ref.shape[1]
    for phase in range(1, SUBLANES):
        sh_ref[phase - 1] = ext_ref[phase:phase + rows, :]
    groups = tt // SUBLANES

    def column_chunk(ci, carry):
        cols = pl.ds(pl.multiple_of(ci * cc, cc), cc)
        acc = None
        for k in range(kw):
            phase = (off + k) % SUBLANES
            a0 = off + k - phase
            src = ext_ref if phase == 0 else sh_ref.at[phase - 1]
            term = src[a0:a0 + tt, cols].reshape(groups, SUBLANES, cc) * w_ref[k, :, cols][None]
            acc = term if acc is None else acc + term
        y_ref[:, cols] = acc.reshape(tt, cc) + b_ref[:, cols]
        return carry

    lax.fori_loop(0, c // cc, column_chunk, 0)
    rc = 2 * SUBLANES
    for r0 in range(0, tt, rc):
        o_ref[r0:r0 + rc, :] = _ln_swish(y_ref[r0:r0 + rc, :], g_ref[...], beta_ref[...]).astype(o_ref.dtype)


def _conv_prompt(u, batch, seq, w_dw, b_dw, ln_g, ln_b, cast=(), layer=0):
    m, c = u.shape
    kw = w_dw.shape[0]
    assert kw - 1 <= CONV_HALO
    tt = _tile(seq, 128)
    assert tt % CONV_HALO == 0
    nt = seq // tt
    hb = tt // CONV_HALO
    vec = lambda a: a.reshape(1, c)
    vspec = pl.BlockSpec((1, c), lambda b, t: (0, 0))
    kern = functools.partial(_conv_prompt_kernel, kw=kw, cc=min(LANES, c), n_cast=len(cast))
    w_rep = jnp.broadcast_to(w_dw[:, None, :], (kw, SUBLANES, c))
    c_in, c_out, c_shape, c_ops = _cast_stream(cast, layer, batch * nt, lambda b, t: b * nt + t)
    outs = pl.pallas_call(
        kern,
        grid=(batch, nt),
        in_specs=[
            pl.BlockSpec((tt, c), lambda b, t: (b * nt + t, 0)),
            pl.BlockSpec((CONV_HALO, c), lambda b, t: (jnp.maximum((b * nt + t) * hb - 1, 0), 0)),
            pl.BlockSpec((kw, SUBLANES, c), lambda b, t: (0, 0, 0)),
            vspec, vspec, vspec,
        ] + c_in,
        out_specs=[pl.BlockSpec((tt, c), lambda b, t: (b * nt + t, 0))] + c_out,
        out_shape=[jax.ShapeDtypeStruct((m, c), BF16)] + c_shape,
        scratch_shapes=[pltpu.VMEM((CONV_HALO + tt, c), F32),
                        pltpu.VMEM((SUBLANES - 1, CONV_HALO + tt - SUBLANES, c), F32),
                        pltpu.VMEM((tt, c), F32)],
        compiler_params=_params("parallel", "parallel"),
        name="conv_prompt",
    )(u, u, w_rep, vec(b_dw), vec(ln_g), vec(ln_b), *c_ops)
    return outs[0] if not cast else outs


def _conv_sample_kernel(st_ref, u_ref, w_ref, b_ref, g_ref, beta_ref, o_ref, ns_ref):
    ks = st_ref.shape[0]
    st = st_ref[...]
    u = u_ref[...]
    y = jnp.sum(st * w_ref[0:ks, :], axis=0, keepdims=True) + u * w_ref[ks:ks + 1, :] + b_ref[...]
    o_ref[...] = _ln_swish(y, g_ref[...], beta_ref[...]).astype(o_ref.dtype)
    ns_ref[0:ks - 1, :] = st_ref[1:ks, :]
    ns_ref[ks - 1:ks, :] = u


def _conv_sample(u, state, w_dw, b_dw, ln_g, ln_b):
    db, c = u.shape
    ks = state.shape[1]
    vec = lambda a: a.reshape(1, c)
    vspec = pl.BlockSpec((1, c), lambda b: (0, 0))
    out, new_state = pl.pallas_call(
        _conv_sample_kernel,
        grid=(db,),
        in_specs=[
            pl.BlockSpec((None, ks, c), lambda b: (b, 0, 0)),
            pl.BlockSpec((None, 1, c), lambda b: (b, 0, 0)),
            pl.BlockSpec((ks + 1, c), lambda b: (0, 0)),
            vspec, vspec, vspec,
        ],
        out_specs=[pl.BlockSpec((None, 1, c), lambda b: (b, 0, 0)), pl.BlockSpec((None, ks, c), lambda b: (b, 0, 0))],
        out_shape=[jax.ShapeDtypeStruct((db, 1, c), F32), jax.ShapeDtypeStruct((db, ks, c), F32)],
        compiler_params=_params("parallel"),
        name="conv_sample",
    )(state, u.reshape(db, 1, c), w_dw, vec(b_dw), vec(ln_g), vec(ln_b))
    return out.reshape(db, c), new_state


def _attn_block_body(jj, q_ref, kb_ref, vt_ref, km_ref, o_ref, scale):
    blk = MOBA_BLOCK
    q = q_ref[jj * blk:(jj + 1) * blk, :]
    qb = q.astype(BF16)
    qs = (q * (scale * LOG2_E)).astype(BF16)
    ln = (jj + 1) * blk
    nt = (((1,), (1,)), ((), ()))
    st = lax.dot_general(kb_ref[0:ln, :], qs, nt, preferred_element_type=F32)
    pieces = []
    if jj > MOBA_TOPK:
        g = lax.dot_general(km_ref[...].astype(BF16), qb, nt, preferred_element_type=F32)
        for n in range(jj):
            cnt = jnp.zeros((1, blk), F32)
            for m in range(jj):
                if m == n:
                    continue
                beats = (g[m:m + 1] >= g[n:n + 1]) if m < n else (g[m:m + 1] > g[n:n + 1])
                cnt = cnt + jnp.where(beats, 1.0, 0.0)
            pieces.append(jnp.where(cnt < MOBA_TOPK, st[n * blk:(n + 1) * blk, :], -jnp.inf))
    else:
        pieces = [st[n * blk:(n + 1) * blk, :] for n in range(jj)]
    key = lax.broadcasted_iota(jnp.int32, (blk, blk), 0)
    qry = lax.broadcasted_iota(jnp.int32, (blk, blk), 1)
    pieces.append(jnp.where(key <= qry, st[jj * blk:ln, :], -jnp.inf))
    mx = jnp.max(pieces[-1], axis=0, keepdims=True)
    for pc in pieces[:-1]:
        mx = jnp.maximum(mx, jnp.max(pc, axis=0, keepdims=True))
    ps = [jnp.exp2(pc - mx) for pc in pieces]
    den = ps[0].sum(axis=0, keepdims=True)
    for p in ps[1:]:
        den = den + p.sum(axis=0, keepdims=True)
    pcat = jnp.concatenate(ps, axis=0) if len(ps) > 1 else ps[0]
    acc_t = jnp.dot(vt_ref[:, 0:ln], pcat.astype(BF16), preferred_element_type=F32)
    o_ref[jj * blk:(jj + 1) * blk, :] = (acc_t / den).T.astype(o_ref.dtype)


def _attn_prompt_kernel(q_ref, k_ref, v_ref, *refs, nb, scale, n_cast, n_after):
    cast_src, refs = refs[:n_cast], refs[n_cast + n_after:]
    o_ref, cast_dst = refs[0], refs[1:1 + n_cast]
    kb_ref, vt_ref, km_ref = refs[1 + n_cast:]
    _cast_rows(cast_src, cast_dst)
    blk = MOBA_BLOCK
    kb_ref[...] = k_ref[...].astype(BF16)
    km_ref[...] = jnp.zeros(km_ref.shape, F32)
    for n in range(nb):
        vt_ref[:, n * blk:(n + 1) * blk] = v_ref[n * blk:(n + 1) * blk, :].T.astype(BF16)
        km_ref[n:n + 1, :] = jnp.mean(k_ref[n * blk:(n + 1) * blk, :], axis=0, keepdims=True)
    for jj in range(nb):
        _attn_block_body(jj, q_ref, kb_ref, vt_ref, km_ref, o_ref, scale)


def _attn_prompt(q, k, v, batch, seq, n_heads, hd, cast=(), layer=0, after=()):
    assert seq % MOBA_BLOCK == 0
    nb = seq // MOBA_BLOCK
    m = batch * seq
    bf16_rows = 2 * SUBLANES
    km_rows = -(-nb // bf16_rows) * bf16_rows
    kern = functools.partial(_attn_prompt_kernel, nb=nb, scale=hd ** -0.5, n_cast=len(cast), n_after=len(after))
    spec = pl.BlockSpec((seq, hd), lambda b, h: (b, h))
    c_in, c_out, c_shape, c_ops = _cast_stream(cast, layer, batch * n_heads, lambda b, h: b * n_heads + h)
    outs = pl.pallas_call(
        kern,
        grid=(batch, n_heads),
        in_specs=[spec, spec, spec] + c_in + [pl.BlockSpec(memory_space=pl.ANY)] * len(after),
        out_specs=[spec] + c_out,
        out_shape=[jax.ShapeDtypeStruct((m, n_heads * hd), BF16)] + c_shape,
        scratch_shapes=[pltpu.VMEM((seq, hd), BF16), pltpu.VMEM((hd, seq), BF16), pltpu.VMEM((km_rows, hd), F32)],
        compiler_params=_params("parallel", "parallel"),
        name="attn_prompt",
    )(q, k, v, *c_ops, *after)
    return outs[0] if not cast else outs


def _topk_kernel(km_ref, q_ref, o_ref):
    nblk = km_ref.shape[0]
    gate = jnp.sum(km_ref[...] * q_ref[...][None], axis=-1, keepdims=True)
    blk_id = lax.broadcasted_iota(jnp.int32, gate.shape, 0)
    for s in range(MOBA_TOPK):
        mx = jnp.max(gate, axis=0, keepdims=True)
        idx = jnp.min(jnp.where(gate == mx, blk_id, nblk), axis=0, keepdims=True)
        o_ref[s:s + 1] = idx
        gate = jnp.where(blk_id == idx, -jnp.inf, gate)


def _sample_topk(kmean, q):
    db, nblk, n_heads, hd = kmean.shape
    assert nblk >= MOBA_TOPK
    out = pl.pallas_call(
        _topk_kernel,
        grid=(db,),
        in_specs=[pl.BlockSpec((None, nblk, n_heads, hd), lambda b: (b, 0, 0, 0)),
                  pl.BlockSpec((None, n_heads, hd), lambda b: (b, 0, 0))],
        out_specs=pl.BlockSpec((None, MOBA_TOPK, n_heads, 1), lambda b: (b, 0, 0, 0)),
        out_shape=jax.ShapeDtypeStruct((db, MOBA_TOPK, n_heads, 1), jnp.int32),
        compiler_params=_params("parallel"),
        name="sample_topk",
    )(kmean, q)
    return jnp.swapaxes(out.reshape(db, MOBA_TOPK, n_heads), 1, 2)


def _attn_sample_kernel(pt_ref, sel_ref, q_ref, kn_ref, vn_ref, ck_hbm, cv_hbm, o_ref, kbuf, vbuf, sems, *,
                        layer, n_pages, ppb, scale):
    b = pl.program_id(0)
    n_heads, hd = q_ref.shape
    n_slab = MOBA_TOPK * ppb

    def slab_copies(h, slot):
        cps = []
        for s in range(MOBA_TOPK):
            blk = sel_ref[(b * n_heads + h) * MOBA_TOPK + s]
            for e in range(ppb):
                page = pt_ref[b * n_pages + blk * ppb + e]
                i = s * ppb + e
                cps.append(pltpu.make_async_copy(ck_hbm.at[layer, page, :, h, :], kbuf.at[slot, i], sems.at[0, slot, i]))
                cps.append(pltpu.make_async_copy(cv_hbm.at[layer, page, :, h, :], vbuf.at[slot, i], sems.at[1, slot, i]))
        return cps

    n_slot = kbuf.shape[0]
    group = n_slot // SAMPLE_GROUPS_BUFFERED
    n_key = n_slab * kbuf.shape[2]

    def group_copies(g):
        return [cp for h in range(g * group, (g + 1) * group) for cp in slab_copies(h, h % n_slot)]

    n_group = n_heads // group
    ahead = n_slot // group - 1
    for g0 in range(min(ahead, n_group)):
        for cp in group_copies(g0):
            cp.start()
    for h in range(n_heads):
        slot = h % n_slot
        if h % group == 0:
            g = h // group
            if g + ahead < n_group:
                for cp in group_copies(g + ahead):
                    cp.start()
            for cp in group_copies(g):
                cp.wait()
        q = q_ref[h:h + 1, :]
        qm = jnp.broadcast_to(q, (LANES, hd)).astype(BF16)
        keys = kbuf[slot].reshape(n_key, hd).astype(BF16)
        s = lax.dot_general(keys, qm, (((1,), (1,)), ((), ())), preferred_element_type=F32) * scale
        s_self = jnp.sum(q * kn_ref[h:h + 1, :], axis=1, keepdims=True) * scale
        mx = jnp.maximum(jnp.max(s, axis=0, keepdims=True), s_self)
        p = jnp.exp(s - mx)
        p_self = jnp.exp(s_self - mx)
        den = jnp.sum(p, axis=0, keepdims=True) + p_self
        acc = jnp.sum(p * vbuf[slot].reshape(n_key, hd), axis=0, keepdims=True) + p_self * vn_ref[h:h + 1, :]
        o_ref[h:h + 1, :] = acc / den


def _attn_sample(q, k_new, v_new, cache_k, cache_v, layer, pt_flat, sel_flat, n_pages, ppb):
    db, n_heads, hd = q.shape
    ps = cache_k.shape[2]
    n_slab = MOBA_TOPK * ppb
    n_slot = _tile(n_heads, SAMPLE_HEAD_GROUP) * SAMPLE_GROUPS_BUFFERED
    assert hd == LANES
    row_spec = pl.BlockSpec((None, n_heads, hd), lambda b, pt, sel: (b, 0, 0))
    any_spec = pl.BlockSpec(memory_space=pl.ANY)
    out = pl.pallas_call(
        functools.partial(_attn_sample_kernel, layer=layer, n_pages=n_pages, ppb=ppb, scale=hd ** -0.5),
        grid_spec=pltpu.PrefetchScalarGridSpec(
            num_scalar_prefetch=2,
            grid=(db,),
            in_specs=[row_spec, row_spec, row_spec, any_spec, any_spec],
            out_specs=row_spec,
            scratch_shapes=[pltpu.VMEM((n_slot, n_slab, ps, hd), F32), pltpu.VMEM((n_slot, n_slab, ps, hd), F32),
                            pltpu.SemaphoreType.DMA((2, n_slot, n_slab))],
        ),
        out_shape=jax.ShapeDtypeStruct((db, n_heads, hd), F32),
        compiler_params=_params("arbitrary"),
        name="attn_sample",
    )(pt_flat, sel_flat, q, k_new, v_new, cache_k, cache_v)
    return out.reshape(db, n_heads * hd)


def _in_projection(z, z_s, w_in, layer, c, a, rope_p, rope_s, table_rows, hd):
    m, ms = z.shape[0], z_s.shape[0]
    tm = _tile(m, 1024)
    tn_glu = _tile(c, 256)
    tn = _tile(a, 512)
    assert table_rows % tm == 0
    period = table_rows // tm
    tbl = [(t, (tm, hd), lambda j, i: (i % period, 0)) for t in rope_p]
    tbl_s = [(t, (ms, hd), lambda j, i: (0, 0)) for t in rope_s]
    col = lambda off, group=0: (0, w_in, layer, 0, off, group)
    u = _matmul([z], [col(0), col(c, 1)], [], _ep_glu, c, tn_glu, F32, "in_proj_glu", side=([z_s], []))
    q = _matmul([z], [col(2 * c)], tbl, _ep_rope, a, tn, F32, "in_proj_q", side=([z_s], tbl_s), after=(u[0],))
    k = _matmul([z], [col(2 * c + a)], tbl, _ep_rope, a, tn, F32, "in_proj_k", side=([z_s], tbl_s), after=(q[0],))
    v = _matmul([z], [col(2 * c + 2 * a)], [], _ep_plain, a, tn, F32, "in_proj_v", side=([z_s], []), after=(k[0],))
    return tuple(zip(u, q, k, v))


def _post_mixer(x, conv_out, attn, p, layer, w_out, g_ffn, w_up, w_down, g_ple, w_ple_gate, w_ple_proj, g_final,
                final_norm, paged=None):
    m, d = x.shape
    c = conv_out.shape[1]
    assert attn.shape[1] == c
    tm = _tile(m, 1024)
    tn = _tile(d, 512)
    resid = lambda r: (r, (tm, tn), lambda j, i: (i, j))
    h = _matmul([conv_out, attn], [(0, w_out, layer, 0, 0, 0), (1, w_out, layer, 1, 0, 0)], [resid(x)], _ep_resid,
                d, tn, F32, "out_proj")
    h = _mlp(h, g_ffn, w_up, w_down, paged)
    if paged is not None:
        h, kmean = h
    out = _ple(h, p, g_ple, g_final, w_ple_gate, w_ple_proj, layer, final_norm)
    return out if paged is None else (out, kmean)


def kernel(x_prompt, x_sample, cache_k, cache_v, state_conv, page_table, p_prompt, p_sample, g_mix, w_in, w_dw, b_dw, g_conv_ln, b_conv_ln, w_out, g_ffn, w_up, w_down, g_ple, w_ple_gate, w_ple_proj, g_final):
    batch, seq, d = x_prompt.shape
    db, dseq, _ = x_sample.shape
    assert dseq == 1
    depth, n_pool, page, n_heads, hd = cache_k.shape
    n_pages = page_table.shape[1]
    past_len = n_pages * page
    assert MOBA_BLOCK % page == 0 and past_len % MOBA_BLOCK == 0
    ppb = MOBA_BLOCK // page
    c = w_dw.shape[-1]
    a = n_heads * hd
    assert c + a == d

    hp = x_prompt.reshape(batch * seq, d)
    hs = x_sample.reshape(db, d)
    cos_p, sin_p = _rope_tables(0, seq, hd)
    cos_s, sin_s = _rope_tables(past_len, 1, hd)
    cos_s = jnp.broadcast_to(cos_s, (db, hd))
    sin_s = jnp.broadcast_to(sin_s, (db, hd))
    pt_flat = page_table.reshape(-1).astype(jnp.int32)

    kp_l, vp_l, cp_l, ks_l, vs_l, cs_l = [], [], [], [], [], []
    for l in range(depth):
        conv_w = (w_dw[l], b_dw[l], g_conv_ln[l], b_conv_ln[l])

        z = _rmsnorm(hp, g_mix[l], BF16)
        z_s = _rmsnorm(hs, g_mix[l], BF16)
        (u, q, k, v), (u_s, q_s, k_s, v_s) = _in_projection(z, z_s, w_in, l, c, a, (cos_p, sin_p), (cos_s, sin_s),
                                                            seq, hd)

        conv_out, w_down_b, w_pg_b = _conv_prompt(u, batch, seq, *conv_w, cast=(w_down, w_ple_gate), layer=l)
        attn, w_up_b = _attn_prompt(q, k, v, batch, seq, n_heads, hd, cast=(w_up,), layer=l, after=(conv_out,))
        post = (l, w_out, g_ffn[l], w_up_b, w_down_b, g_ple[l], w_pg_b, w_ple_proj, g_final, l == depth - 1)
        hp, kmean = _post_mixer(hp, conv_out, attn, p_prompt[l].reshape(batch * seq, -1), *post,
                                paged=(cache_k, l, pt_flat, ppb))
        kp_l.append(k.reshape(batch, seq, n_heads, hd))
        vp_l.append(v.reshape(batch, seq, n_heads, hd))
        cp_l.append(u.reshape(batch, seq, c)[:, seq - (w_dw.shape[1] - 1):, :])

        conv_out, new_state = _conv_sample(u_s, state_conv[l], *conv_w)
        h3 = lambda t: t.reshape(db, n_heads, hd)
        sel = _sample_topk(kmean.reshape(db, n_pages // ppb, n_heads, hd), h3(q_s))
        attn = _attn_sample(h3(q_s), h3(k_s), h3(v_s), cache_k, cache_v, l, pt_flat, sel.reshape(-1), n_pages, ppb)
        hs = _post_mixer(hs, conv_out.astype(BF16), attn.astype(BF16), p_sample[l].reshape(db, -1), *post)
        ks_l.append(k_s.reshape(db, 1, n_heads, hd))
        vs_l.append(v_s.reshape(db, 1, n_heads, hd))
        cs_l.append(new_state)

    y_prompt = hp.reshape(batch, seq, d)
    y_sample = hs.reshape(db, 1, d)
    stack = lambda xs: xs[0][None] if len(xs) == 1 else jnp.stack(xs)
    return (y_prompt, y_sample, stack(kp_l), stack(vp_l), stack(cp_l), stack(ks_l), stack(vs_l), stack(cs_l))
```

```python
import functools

import jax
import jax.numpy as jnp
from jax import lax
from jax.experimental import pallas as pl
from jax.experimental.pallas import tpu as pltpu

F32 = jnp.float32
BF16 = jnp.bfloat16

MOBA_BLOCK = 256
MOBA_TOPK = 3
ROPE_THETA = 10000.0
NORM_EPS = 1e-6
LOG2_E = 1.4426950408889634
LANES = 128
SUBLANES = 8
CONV_HALO = 32
MM_ROW_CHUNK = 512
NORM_ROW_CHUNK = 64
SAMPLE_HEAD_GROUP = 4
SAMPLE_GROUPS_BUFFERED = 3
VMEM_LIMIT = 60 * 1024 * 1024


def _params(*sem):
    return pltpu.CompilerParams(dimension_semantics=sem, vmem_limit_bytes=VMEM_LIMIT)


def _tile(n, pref):
    if n <= pref:
        return n
    t = pref
    while n % t:
        t //= 2
    return t


def _rmsnorm_kernel(x_ref, g_ref, o_ref):
    x = x_ref[...]
    ms = jnp.mean(x * x, axis=-1, keepdims=True)
    o_ref[...] = (x * lax.rsqrt(ms + NORM_EPS) * g_ref[...]).astype(o_ref.dtype)


def _rmsnorm(x, g, out_dtype):
    m, d = x.shape
    tm = _tile(m, 256)
    return pl.pallas_call(
        _rmsnorm_kernel,
        grid=(m // tm,),
        in_specs=[pl.BlockSpec((tm, d), lambda i: (i, 0)), pl.BlockSpec((1, d), lambda i: (0, 0))],
        out_specs=pl.BlockSpec((tm, d), lambda i: (i, 0)),
        out_shape=jax.ShapeDtypeStruct((m, d), out_dtype),
        compiler_params=_params("parallel"),
        name="rmsnorm",
    )(x, g.reshape(1, d))


def _mm_kernel(*refs, n_lhs, dots, n_extra, epilogue, side, n_after):
    n_dot = len(dots)
    groups = 2 if side else 1
    pos = 0
    lhs = [refs[pos + g * n_lhs:pos + (g + 1) * n_lhs] for g in range(groups)]
    pos += groups * n_lhs
    ws = refs[pos:pos + n_dot]
    pos += n_dot
    extras = [refs[pos + g * n_extra:pos + (g + 1) * n_extra] for g in range(groups)]
    pos += groups * n_extra + n_after
    outs = refs[pos:pos + groups]
    wbs = refs[pos + groups:pos + groups + n_dot]
    i = pl.program_id(1)

    @pl.when(i == 0)
    def _():
        for w, wb in zip(ws, wbs):
            wb[...] = w[...].astype(BF16)

    def run(g):
        rows = outs[g].shape[0]
        rc = MM_ROW_CHUNK if rows % MM_ROW_CHUNK == 0 else rows
        for r0 in range(0, rows, rc):
            accs = {}
            for (li, group), wb in zip(dots, wbs):
                a = lhs[g][li][r0:r0 + rc, :]
                if a.dtype != BF16:
                    a = a.astype(BF16)
                d = jnp.dot(a, wb[...], preferred_element_type=F32)
                accs[group] = d if group not in accs else accs[group] + d
            res = epilogue([accs[k] for k in sorted(accs)], [e[r0:r0 + rc, :] for e in extras[g]])
            outs[g][r0:r0 + rc, :] = res.astype(outs[g].dtype)

    run(0)
    if side:
        pl.when(i == pl.num_programs(1) - 1)(functools.partial(run, 1))


def _matmul(lhs, dots, extras, epilogue, n_cols, tn, out_dtype, name, side=None, tm_pref=1024, after=()):
    m = lhs[0].shape[0]
    tm = _tile(m, tm_pref)
    in_specs = [pl.BlockSpec((tm, a.shape[1]), lambda j, i: (i, 0)) for a in lhs]
    operands = list(lhs)
    if side:
        lhs_s, extras_s = side
        ms = lhs_s[0].shape[0]
        in_specs += [pl.BlockSpec((ms, a.shape[1]), lambda j, i: (0, 0)) for a in lhs_s]
        operands += list(lhs_s)
    scratch = []
    for li, w, layer, rb, co, _ in dots:
        kl = lhs[li].shape[1]
        assert co % tn == 0
        in_specs.append(pl.BlockSpec(
            (None, kl, tn), functools.partial(lambda j, i, layer, rb, cb: (layer, rb, cb + j), layer=layer, rb=rb,
                                              cb=co // tn)))
        operands.append(w)
        scratch.append(pltpu.VMEM((kl, tn), BF16))
    for arr, blk, imap in list(extras) + (list(extras_s) if side else []):
        in_specs.append(pl.BlockSpec(blk, imap))
        operands.append(arr)
    in_specs += [pl.BlockSpec(memory_space=pl.ANY)] * len(after)
    operands += list(after)
    out_specs = [pl.BlockSpec((tm, tn), lambda j, i: (i, j))]
    out_shape = [jax.ShapeDtypeStruct((m, n_cols), out_dtype)]
    if side:
        out_specs.append(pl.BlockSpec((ms, tn), lambda j, i: (0, j)))
        out_shape.append(jax.ShapeDtypeStruct((ms, n_cols), out_dtype))
    kern = functools.partial(_mm_kernel, n_lhs=len(lhs), dots=[(d[0], d[5]) for d in dots],
                             n_extra=len(extras), epilogue=epilogue, side=bool(side), n_after=len(after))
    outs = pl.pallas_call(
        kern,
        grid=(n_cols // tn, m // tm),
        in_specs=in_specs,
        out_specs=out_specs,
        out_shape=out_shape,
        scratch_shapes=scratch,
        compiler_params=_params("parallel", "arbitrary"),
        name=name,
    )(*operands)
    return outs if side else outs[0]


def _ep_plain(accs, extras):
    return accs[0]


def _ep_glu(accs, extras):
    return accs[0] * jax.nn.sigmoid(accs[1])


def _ep_resid(accs, extras):
    return extras[0] + accs[0]


def _ep_rope(accs, extras):
    cos, sin_signed = extras
    y = accs[0]
    hd = cos.shape[1]
    outs = []
    for h in range(y.shape[1] // hd):
        xh = y[:, h * hd:(h + 1) * hd]
        outs.append(xh * cos + pltpu.roll(xh, hd // 2, 1) * sin_signed)
    return jnp.concatenate(outs, axis=1)


def _rope_tables(pos0, s, hd):
    inv = 1.0 / (ROPE_THETA ** (jnp.arange(0, hd, 2, dtype=F32) / hd))
    pos = (jnp.arange(s, dtype=jnp.int32) + pos0).astype(F32)
    ang = pos[:, None] * inv[None, :]
    cos, sin = jnp.cos(ang), jnp.sin(ang)
    return jnp.concatenate([cos, cos], axis=-1), jnp.concatenate([-sin, sin], axis=-1)


def _rms_scale(x, g):
    ms = jnp.mean(x * x, axis=-1, keepdims=True)
    return x * lax.rsqrt(ms + NORM_EPS) * g


def _mlp_kernel(pt_ref, h_ref, g_ref, wu_ref, wd_ref, *refs, n_page, ppb):
    page_refs = refs[:n_page]
    o_ref = refs[n_page]
    z_ref = refs[-1]
    f = pl.program_id(1)

    @pl.when(f == 0)
    def _():
        rows = h_ref.shape[0]
        rc = _tile(rows, NORM_ROW_CHUNK)
        for r0 in range(0, rows, rc):
            h = h_ref[r0:r0 + rc, :]
            o_ref[r0:r0 + rc, :] = h
            z_ref[r0:r0 + rc, :] = _rms_scale(h, g_ref[...]).astype(BF16)

    hm = jnp.dot(z_ref[...], wu_ref[...], preferred_element_type=F32)
    hm = jnp.square(jnp.maximum(hm, 0.0)).astype(BF16)
    o_ref[...] += jnp.dot(hm, wd_ref[...], preferred_element_type=F32)

    if n_page:
        km_ref = refs[n_page + 1]
        ps = page_refs[0].shape[0]
        for r in range(n_page // ppb):
            s = jnp.sum(page_refs[r * ppb][...], axis=0)
            for e in range(1, ppb):
                s = s + jnp.sum(page_refs[r * ppb + e][...], axis=0)
            km_ref[r] = s * (1.0 / (ppb * ps))


def _mlp(h, g, w_up, w_down, paged=None):
    m, d = h.shape
    dff = w_up.shape[1]
    tm = _tile(m, 512)
    tf = _tile(dff, 512)
    nf = dff // tf
    steps = (m // tm) * nf
    in_specs = [
        pl.BlockSpec((tm, d), lambda i, f, pt: (i, 0), pipeline_mode=pl.Buffered(1)),
        pl.BlockSpec((1, d), lambda i, f, pt: (0, 0)),
        pl.BlockSpec((d, tf), lambda i, f, pt: (0, f)),
        pl.BlockSpec((tf, d), lambda i, f, pt: (f, 0)),
    ]
    out_specs = [pl.BlockSpec((tm, d), lambda i, f, pt: (i, 0))]
    out_shape = [jax.ShapeDtypeStruct((m, d), F32)]
    operands = [h, g.reshape(1, d), w_up, w_down]
    n_page, ppb = 0, 1
    if paged is None:
        pt_flat = jnp.zeros((1,), jnp.int32)
    else:
        cache_k, layer, pt_flat, ppb = paged
        _, _, ps, n_heads, hd = cache_k.shape
        n_slots = pt_flat.shape[0]
        assert n_slots % (steps * ppb) == 0
        n_page = n_slots // steps
        for e in range(n_page):
            in_specs.append(pl.BlockSpec(
                (None, None, ps, n_heads, hd),
                functools.partial(lambda i, f, pt, e: (layer, pt[(i * nf + f) * n_page + e], 0, 0, 0), e=e)))
        operands += [cache_k] * n_page
        out_specs.append(pl.BlockSpec((n_page // ppb, n_heads, hd), lambda i, f, pt: (i * nf + f, 0, 0)))
        out_shape.append(jax.ShapeDtypeStruct((n_slots // ppb, n_heads, hd), F32))
    outs = pl.pallas_call(
        functools.partial(_mlp_kernel, n_page=n_page, ppb=ppb),
        grid_spec=pltpu.PrefetchScalarGridSpec(
            num_scalar_prefetch=1, grid=(m // tm, nf), in_specs=in_specs, out_specs=out_specs,
            scratch_shapes=[pltpu.VMEM((tm, d), BF16)]),
        out_shape=out_shape,
        compiler_params=_params("parallel", "arbitrary"),
        name="mlp",
    )(pt_flat, *operands)
    return outs[0] if paged is None else outs


def _ple_kernel(h_ref, p_ref, g_ref, gf_ref, wg_ref, wp_ref, y_ref, z_ref, pb_ref, *, final_norm):
    j = pl.program_id(1)
    tn = wg_ref.shape[1]

    rows = h_ref.shape[0]
    rc = _tile(rows, NORM_ROW_CHUNK)

    @pl.when(j == 0)
    def _():
        for r0 in range(0, rows, rc):
            z_ref[r0:r0 + rc, :] = _rms_scale(h_ref[r0:r0 + rc, :], g_ref[...]).astype(BF16)
        pb_ref[...] = p_ref[...].astype(BF16)

    cols = pl.ds(pl.multiple_of(j * tn, tn), tn)
    gate = jnp.dot(z_ref[...], wg_ref[...], preferred_element_type=F32)
    emb = jnp.dot(pb_ref[...], wp_ref[...].astype(BF16), preferred_element_type=F32)
    y_ref[:, cols] = h_ref[:, cols] + jax.nn.sigmoid(gate) * emb

    if final_norm:
        @pl.when(j == pl.num_programs(1) - 1)
        def _():
            for r0 in range(0, rows, rc):
                y_ref[r0:r0 + rc, :] = _rms_scale(y_ref[r0:r0 + rc, :], gf_ref[...])


def _ple(h, p, g_ple, g_final, w_gate, w_proj, layer, final_norm):
    m, d = h.shape
    pdim = p.shape[1]
    tm = _tile(m, 512)
    tn = _tile(d, 1024)
    vec = pl.BlockSpec((1, d), lambda i, j: (0, 0))
    return pl.pallas_call(
        functools.partial(_ple_kernel, final_norm=final_norm),
        grid=(m // tm, d // tn),
        in_specs=[
            pl.BlockSpec((tm, d), lambda i, j: (i, 0), pipeline_mode=pl.Buffered(1)),
            pl.BlockSpec((tm, pdim), lambda i, j: (i, 0)),
            vec, vec,
            pl.BlockSpec((d, tn), lambda i, j: (0, j)),
            pl.BlockSpec((None, pdim, tn), lambda i, j: (layer, 0, j)),
        ],
        out_specs=pl.BlockSpec((tm, d), lambda i, j: (i, 0)),
        out_shape=jax.ShapeDtypeStruct((m, d), F32),
        scratch_shapes=[pltpu.VMEM((tm, d), BF16), pltpu.VMEM((tm, pdim), BF16)],
        compiler_params=_params("parallel", "arbitrary"),
        name="ple",
    )(h, p, g_ple.reshape(1, d), g_final.reshape(1, d), w_gate, w_proj)


def _cast_stream(weights, layer, steps, step_of):
    in_specs, out_specs, out_shapes = [], [], []
    for w in weights:
        _, r, c = w.shape
        rows = r // steps
        assert r % steps == 0 and rows % (2 * SUBLANES) == 0
        in_specs.append(pl.BlockSpec((None, rows, c), lambda *g: (layer, step_of(*g), 0)))
        out_specs.append(pl.BlockSpec((rows, c), lambda *g: (step_of(*g), 0)))
        out_shapes.append(jax.ShapeDtypeStruct((r, c), BF16))
    return in_specs, out_specs, out_shapes, list(weights)


def _cast_rows(src_refs, dst_refs):
    for src, dst in zip(src_refs, dst_refs):
        dst[...] = src[...].astype(BF16)


def _ln_swish(y, g, beta):
    mu = jnp.mean(y, axis=-1, keepdims=True)
    xc = y - mu
    var = jnp.mean(xc * xc, axis=-1, keepdims=True)
    yn = xc * lax.rsqrt(var + NORM_EPS) * g + beta
    return yn * jax.nn.sigmoid(yn)


def _conv_prompt_kernel(cur_ref, halo_ref, w_ref, b_ref, g_ref, beta_ref, *refs, kw, cc, n_cast):
    cast_src, refs = refs[:n_cast], refs[n_cast:]
    o_ref, cast_dst = refs[0], refs[1:1 + n_cast]
    ext_ref, sh_ref, y_ref = refs[1 + n_cast:]
    _cast_rows(cast_src, cast_dst)
    t = pl.program_id(1)
    tt, c = cur_ref.shape
    ext_ref[0:CONV_HALO, :] = jnp.where(t == 0, 0.0, halo_ref[...])
    ext_ref[CONV_HALO:CONV_HALO + tt, :] = cur_ref[...]
    off = CONV_HALO - (kw - 1)
    rows = sh_ref.shape[1]
    for phase in range(1, SUBLANES):
        sh_ref[phase - 1] = ext_ref[phase:phase + rows, :]
    groups = tt // SUBLANES

    def column_chunk(ci, carry):
        cols = pl.ds(pl.multiple_of(ci * cc, cc), cc)
        acc = None
        for k in range(kw):
            phase = (off + k) % SUBLANES
            a0 = off + k - phase
            src = ext_ref if phase == 0 else sh_ref.at[phase - 1]
            term = src[a0:a0 + tt, cols].reshape(groups, SUBLANES, cc) * w_ref[k, :, cols][None]
            acc = term if acc is None else acc + term
        y_ref[:, cols] = acc.reshape(tt, cc) + b_ref[:, cols]
        return carry

    lax.fori_loop(0, c // cc, column_chunk, 0)
    rc = 2 * SUBLANES
    for r0 in range(0, tt, rc):
        o_ref[r0:r0 + rc, :] = _ln_swish(y_ref[r0:r0 + rc, :], g_ref[...], beta_ref[...]).astype(o_ref.dtype)


def _conv_prompt(u, batch, seq, w_dw, b_dw, ln_g, ln_b, cast=(), layer=0):
    m, c = u.shape
    kw = w_dw.shape[0]
    assert kw - 1 <= CONV_HALO
    tt = _tile(seq, 128)
    assert tt % CONV_HALO == 0
    nt = seq // tt
    hb = tt // CONV_HALO
    vec = lambda a: a.reshape(1, c)
    vspec = pl.BlockSpec((1, c), lambda b, t: (0, 0))
    kern = functools.partial(_conv_prompt_kernel, kw=kw, cc=min(LANES, c), n_cast=len(cast))
    w_rep = jnp.broadcast_to(w_dw[:, None, :], (kw, SUBLANES, c))
    c_in, c_out, c_shape, c_ops = _cast_stream(cast, layer, batch * nt, lambda b, t: b * nt + t)
    outs = pl.pallas_call(
        kern,
        grid=(batch, nt),
        in_specs=[
            pl.BlockSpec((tt, c), lambda b, t: (b * nt + t, 0)),
            pl.BlockSpec((CONV_HALO, c), lambda b, t: (jnp.maximum((b * nt + t) * hb - 1, 0), 0)),
            pl.BlockSpec((kw, SUBLANES, c), lambda b, t: (0, 0, 0)),
            vspec, vspec, vspec,
        ] + c_in,
        out_specs=[pl.BlockSpec((tt, c), lambda b, t: (b * nt + t, 0))] + c_out,
        out_shape=[jax.ShapeDtypeStruct((m, c), BF16)] + c_shape,
        scratch_shapes=[pltpu.VMEM((CONV_HALO + tt, c), F32),
                        pltpu.VMEM((SUBLANES - 1, CONV_HALO + tt - SUBLANES, c), F32),
                        pltpu.VMEM((tt, c), F32)],
        compiler_params=_params("parallel", "parallel"),
        name="conv_prompt",
    )(u, u, w_rep, vec(b_dw), vec(ln_g), vec(ln_b), *c_ops)
    return outs[0] if not cast else outs


def _conv_sample_kernel(st_ref, u_ref, w_ref, b_ref, g_ref, beta_ref, o_ref, ns_ref):
    ks = st_ref.shape[0]
    st = st_ref[...]
    u = u_ref[...]
    y = jnp.sum(st * w_ref[0:ks, :], axis=0, keepdims=True) + u * w_ref[ks:ks + 1, :] + b_ref[...]
    o_ref[...] = _ln_swish(y, g_ref[...], beta_ref[...]).astype(o_ref.dtype)
    ns_ref[0:ks - 1, :] = st_ref[1:ks, :]
    ns_ref[ks - 1:ks, :] = u


def _conv_sample(u, state, w_dw, b_dw, ln_g, ln_b):
    db, c = u.shape
    ks = state.shape[1]
    vec = lambda a: a.reshape(1, c)
    vspec = pl.BlockSpec((1, c), lambda b: (0, 0))
    out, new_state = pl.pallas_call(
        _conv_sample_kernel,
        grid=(db,),
        in_specs=[
            pl.BlockSpec((None, ks, c), lambda b: (b, 0, 0)),
            pl.BlockSpec((None, 1, c), lambda b: (b, 0, 0)),
            pl.BlockSpec((ks + 1, c), lambda b: (0, 0)),
            vspec, vspec, vspec,
        ],
        out_specs=[pl.BlockSpec((None, 1, c), lambda b: (b, 0, 0)), pl.BlockSpec((None, ks, c), lambda b: (b, 0, 0))],
        out_shape=[jax.ShapeDtypeStruct((db, 1, c), F32), jax.ShapeDtypeStruct((db, ks, c), F32)],
        compiler_params=_params("parallel"),
        name="conv_sample",
    )(state, u.reshape(db, 1, c), w_dw, vec(b_dw), vec(ln_g), vec(ln_b))
    return out.reshape(db, c), new_state


def _attn_block_body(jj, q_ref, kb_ref, vt_ref, km_ref, o_ref, scale):
    blk = MOBA_BLOCK
    q = q_ref[jj * blk:(jj + 1) * blk, :]
    qb = q.astype(BF16)
    qs = (q * (scale * LOG2_E)).astype(BF16)
    ln = (jj + 1) * blk
    nt = (((1,), (1,)), ((), ()))
    st = lax.dot_general(kb_ref[0:ln, :], qs, nt, preferred_element_type=F32)
    pieces = []
    if jj > MOBA_TOPK:
        g = lax.dot_general(km_ref[...].astype(BF16), qb, nt, preferred_element_type=F32)
        for n in range(jj):
            cnt = jnp.zeros((1, blk), F32)
            for m in range(jj):
                if m == n:
                    continue
                beats = (g[m:m + 1] >= g[n:n + 1]) if m < n else (g[m:m + 1] > g[n:n + 1])
                cnt = cnt + jnp.where(beats, 1.0, 0.0)
            pieces.append(jnp.where(cnt < MOBA_TOPK, st[n * blk:(n + 1) * blk, :], -jnp.inf))
    else:
        pieces = [st[n * blk:(n + 1) * blk, :] for n in range(jj)]
    key = lax.broadcasted_iota(jnp.int32, (blk, blk), 0)
    qry = lax.broadcasted_iota(jnp.int32, (blk, blk), 1)
    pieces.append(jnp.where(key <= qry, st[jj * blk:ln, :], -jnp.inf))
    mx = jnp.max(pieces[-1], axis=0, keepdims=True)
    for pc in pieces[:-1]:
        mx = jnp.maximum(mx, jnp.max(pc, axis=0, keepdims=True))
    ps = [jnp.exp2(pc - mx) for pc in pieces]
    den = ps[0].sum(axis=0, keepdims=True)
    for p in ps[1:]:
        den = den + p.sum(axis=0, keepdims=True)
    pcat = jnp.concatenate(ps, axis=0) if len(ps) > 1 else ps[0]
    acc_t = jnp.dot(vt_ref[:, 0:ln], pcat.astype(BF16), preferred_element_type=F32)
    o_ref[jj * blk:(jj + 1) * blk, :] = (acc_t / den).T.astype(o_ref.dtype)


def _attn_prompt_kernel(q_ref, k_ref, v_ref, *refs, nb, scale, n_cast, n_after):
    cast_src, refs = refs[:n_cast], refs[n_cast + n_after:]
    o_ref, cast_dst = refs[0], refs[1:1 + n_cast]
    kb_ref, vt_ref, km_ref = refs[1 + n_cast:]
    _cast_rows(cast_src, cast_dst)
    blk = MOBA_BLOCK
    kb_ref[...] = k_ref[...].astype(BF16)
    km_ref[...] = jnp.zeros(km_ref.shape, F32)
    for n in range(nb):
        vt_ref[:, n * blk:(n + 1) * blk] = v_ref[n * blk:(n + 1) * blk, :].T.astype(BF16)
        km_ref[n:n + 1, :] = jnp.mean(k_ref[n * blk:(n + 1) * blk, :], axis=0, keepdims=True)
    for jj in range(nb):
        _attn_block_body(jj, q_ref, kb_ref, vt_ref, km_ref, o_ref, scale)


def _attn_prompt(q, k, v, batch, seq, n_heads, hd, cast=(), layer=0, after=()):
    assert seq % MOBA_BLOCK == 0
    nb = seq // MOBA_BLOCK
    m = batch * seq
    bf16_rows = 2 * SUBLANES
    km_rows = -(-nb // bf16_rows) * bf16_rows
    kern = functools.partial(_attn_prompt_kernel, nb=nb, scale=hd ** -0.5, n_cast=len(cast), n_after=len(after))
    spec = pl.BlockSpec((seq, hd), lambda b, h: (b, h))
    c_in, c_out, c_shape, c_ops = _cast_stream(cast, layer, batch * n_heads, lambda b, h: b * n_heads + h)
    outs = pl.pallas_call(
        kern,
        grid=(batch, n_heads),
        in_specs=[spec, spec, spec] + c_in + [pl.BlockSpec(memory_space=pl.ANY)] * len(after),
        out_specs=[spec] + c_out,
        out_shape=[jax.ShapeDtypeStruct((m, n_heads * hd), BF16)] + c_shape,
        scratch_shapes=[pltpu.VMEM((seq, hd), BF16), pltpu.VMEM((hd, seq), BF16), pltpu.VMEM((km_rows, hd), F32)],
        compiler_params=_params("parallel", "parallel"),
        name="attn_prompt",
    )(q, k, v, *c_ops, *after)
    return outs[0] if not cast else outs


def _topk_kernel(km_ref, q_ref, o_ref):
    nblk = km_ref.shape[0]
    gate = jnp.sum(km_ref[...] * q_ref[...][None], axis=-1, keepdims=True)
    blk_id = lax.broadcasted_iota(jnp.int32, gate.shape, 0)
    for s in range(MOBA_TOPK):
        mx = jnp.max(gate, axis=0, keepdims=True)
        idx = jnp.min(jnp.where(gate == mx, blk_id, nblk), axis=0, keepdims=True)
        o_ref[s:s + 1] = idx
        gate = jnp.where(blk_id == idx, -jnp.inf, gate)


def _sample_topk(kmean, q):
    db, nblk, n_heads, hd = kmean.shape
    assert nblk >= MOBA_TOPK
    out = pl.pallas_call(
        _topk_kernel,
        grid=(db,),
        in_specs=[pl.BlockSpec((None, nblk, n_heads, hd), lambda b: (b, 0, 0, 0)),
                  pl.BlockSpec((None, n_heads, hd), lambda b: (b, 0, 0))],
        out_specs=pl.BlockSpec((None, MOBA_TOPK, n_heads, 1), lambda b: (b, 0, 0, 0)),
        out_shape=jax.ShapeDtypeStruct((db, MOBA_TOPK, n_heads, 1), jnp.int32),
        compiler_params=_params("parallel"),
        name="sample_topk",
    )(kmean, q)
    return jnp.swapaxes(out.reshape(db, MOBA_TOPK, n_heads), 1, 2)


def _attn_sample_kernel(pt_ref, sel_ref, q_ref, kn_ref, vn_ref, ck_hbm, cv_hbm, o_ref, kbuf, vbuf, sems, *,
                        layer, n_pages, ppb, scale):
    b = pl.program_id(0)
    n_heads, hd = q_ref.shape
    n_slab = MOBA_TOPK * ppb

    def slab_copies(h, slot):
        cps = []
        for s in range(MOBA_TOPK):
            blk = sel_ref[(b * n_heads + h) * MOBA_TOPK + s]
            for e in range(ppb):
                page = pt_ref[b * n_pages + blk * ppb + e]
                i = s * ppb + e
                cps.append(pltpu.make_async_copy(ck_hbm.at[layer, page, :, h, :], kbuf.at[slot, i], sems.at[0, slot, i]))
                cps.append(pltpu.make_async_copy(cv_hbm.at[layer, page, :, h, :], vbuf.at[slot, i], sems.at[1, slot, i]))
        return cps

    n_slot = kbuf.shape[0]
    group = n_slot // SAMPLE_GROUPS_BUFFERED
    n_key = n_slab * kbuf.shape[2]

    def group_copies(g):
        return [cp for h in range(g * group, (g + 1) * group) for cp in slab_copies(h, h % n_slot)]

    n_group = n_heads // group
    ahead = n_slot // group - 1
    def start_group(g):
        for n, cp in enumerate(group_copies(g)):
            cp.start(priority=n % 2)

    for g0 in range(min(ahead, n_group)):
        start_group(g0)
    for h in range(n_heads):
        slot = h % n_slot
        if h % group == 0:
            g = h // group
            if g + ahead < n_group:
                start_group(g + ahead)
            for cp in group_copies(g):
                cp.wait()
        q = q_ref[h:h + 1, :]
        qm = jnp.broadcast_to(q, (LANES, hd)).astype(BF16)
        keys = kbuf[slot].reshape(n_key, hd).astype(BF16)
        s = lax.dot_general(keys, qm, (((1,), (1,)), ((), ())), preferred_element_type=F32) * scale
        s_self = jnp.sum(q * kn_ref[h:h + 1, :], axis=1, keepdims=True) * scale
        mx = jnp.maximum(jnp.max(s, axis=0, keepdims=True), s_self)
        p = jnp.exp(s - mx)
        p_self = jnp.exp(s_self - mx)
        den = jnp.sum(p, axis=0, keepdims=True) + p_self
        acc = jnp.sum(p * vbuf[slot].reshape(n_key, hd), axis=0, keepdims=True) + p_self * vn_ref[h:h + 1, :]
        o_ref[h:h + 1, :] = acc / den


def _attn_sample(q, k_new, v_new, cache_k, cache_v, layer, pt_flat, sel_flat, n_pages, ppb):
    db, n_heads, hd = q.shape
    ps = cache_k.shape[2]
    n_slab = MOBA_TOPK * ppb
    n_slot = _tile(n_heads, SAMPLE_HEAD_GROUP) * SAMPLE_GROUPS_BUFFERED
    assert hd == LANES
    row_spec = pl.BlockSpec((None, n_heads, hd), lambda b, pt, sel: (b, 0, 0))
    any_spec = pl.BlockSpec(memory_space=pl.ANY)
    out = pl.pallas_call(
        functools.partial(_attn_sample_kernel, layer=layer, n_pages=n_pages, ppb=ppb, scale=hd ** -0.5),
        grid_spec=pltpu.PrefetchScalarGridSpec(
            num_scalar_prefetch=2,
            grid=(db,),
            in_specs=[row_spec, row_spec, row_spec, any_spec, any_spec],
            out_specs=row_spec,
            scratch_shapes=[pltpu.VMEM((n_slot, n_slab, ps, hd), F32), pltpu.VMEM((n_slot, n_slab, ps, hd), F32),
                            pltpu.SemaphoreType.DMA((2, n_slot, n_slab))],
        ),
        out_shape=jax.ShapeDtypeStruct((db, n_heads, hd), F32),
        compiler_params=_params("arbitrary"),
        name="attn_sample",
    )(pt_flat, sel_flat, q, k_new, v_new, cache_k, cache_v)
    return out.reshape(db, n_heads * hd)


def _in_projection(z, z_s, w_in, layer, c, a, rope_p, rope_s, table_rows, hd):
    m, ms = z.shape[0], z_s.shape[0]
    tm = _tile(m, 1024)
    tn_glu = _tile(c, 256)
    tn = _tile(a, 512)
    assert table_rows % tm == 0
    period = table_rows // tm
    tbl = [(t, (tm, hd), lambda j, i: (i % period, 0)) for t in rope_p]
    tbl_s = [(t, (ms, hd), lambda j, i: (0, 0)) for t in rope_s]
    col = lambda off, group=0: (0, w_in, layer, 0, off, group)
    u = _matmul([z], [col(0), col(c, 1)], [], _ep_glu, c, tn_glu, F32, "in_proj_glu", side=([z_s], []))
    q = _matmul([z], [col(2 * c)], tbl, _ep_rope, a, tn, F32, "in_proj_q", side=([z_s], tbl_s), after=(u[0],))
    k = _matmul([z], [col(2 * c + a)], tbl, _ep_rope, a, tn, F32, "in_proj_k", side=([z_s], tbl_s), after=(q[0],))
    v = _matmul([z], [col(2 * c + 2 * a)], [], _ep_plain, a, tn, F32, "in_proj_v", side=([z_s], []), after=(k[0],))
    return tuple(zip(u, q, k, v))


def _post_mixer(x, conv_out, attn, p, layer, w_out, g_ffn, w_up, w_down, g_ple, w_ple_gate, w_ple_proj, g_final,
                final_norm, paged=None):
    m, d = x.shape
    c = conv_out.shape[1]
    assert attn.shape[1] == c
    tm = _tile(m, 1024)
    tn = _tile(d, 512)
    resid = lambda r: (r, (tm, tn), lambda j, i: (i, j))
    h = _matmul([conv_out, attn], [(0, w_out, layer, 0, 0, 0), (1, w_out, layer, 1, 0, 0)], [resid(x)], _ep_resid,
                d, tn, F32, "out_proj")
    h = _mlp(h, g_ffn, w_up, w_down, paged)
    if paged is not None:
        h, kmean = h
    out = _ple(h, p, g_ple, g_final, w_ple_gate, w_ple_proj, layer, final_norm)
    return out if paged is None else (out, kmean)


def kernel(x_prompt, x_sample, cache_k, cache_v, state_conv, page_table, p_prompt, p_sample, g_mix, w_in, w_dw, b_dw, g_conv_ln, b_conv_ln, w_out, g_ffn, w_up, w_down, g_ple, w_ple_gate, w_ple_proj, g_final):
    batch, seq, d = x_prompt.shape
    db, dseq, _ = x_sample.shape
    assert dseq == 1
    depth, n_pool, page, n_heads, hd = cache_k.shape
    n_pages = page_table.shape[1]
    past_len = n_pages * page
    assert MOBA_BLOCK % page == 0 and past_len % MOBA_BLOCK == 0
    ppb = MOBA_BLOCK // page
    c = w_dw.shape[-1]
    a = n_heads * hd
    assert c + a == d

    hp = x_prompt.reshape(batch * seq, d)
    hs = x_sample.reshape(db, d)
    cos_p, sin_p = _rope_tables(0, seq, hd)
    cos_s, sin_s = _rope_tables(past_len, 1, hd)
    cos_s = jnp.broadcast_to(cos_s, (db, hd))
    sin_s = jnp.broadcast_to(sin_s, (db, hd))
    pt_flat = page_table.reshape(-1).astype(jnp.int32)

    kp_l, vp_l, cp_l, ks_l, vs_l, cs_l = [], [], [], [], [], []
    for l in range(depth):
        conv_w = (w_dw[l], b_dw[l], g_conv_ln[l], b_conv_ln[l])

        z = _rmsnorm(hp, g_mix[l], BF16)
        z_s = _rmsnorm(hs, g_mix[l], BF16)
        (u, q, k, v), (u_s, q_s, k_s, v_s) = _in_projection(z, z_s, w_in, l, c, a, (cos_p, sin_p), (cos_s, sin_s),
                                                            seq, hd)

        conv_out, w_down_b, w_pg_b = _conv_prompt(u, batch, seq, *conv_w, cast=(w_down, w_ple_gate), layer=l)
        attn, w_up_b = _attn_prompt(q, k, v, batch, seq, n_heads, hd, cast=(w_up,), layer=l, after=(conv_out,))
        post = (l, w_out, g_ffn[l], w_up_b, w_down_b, g_ple[l], w_pg_b, w_ple_proj, g_final, l == depth - 1)
        hp, kmean = _post_mixer(hp, conv_out, attn, p_prompt[l].reshape(batch * seq, -1), *post,
                                paged=(cache_k, l, pt_flat, ppb))
        kp_l.append(k.reshape(batch, seq, n_heads, hd))
        vp_l.append(v.reshape(batch, seq, n_heads, hd))
        cp_l.append(u.reshape(batch, seq, c)[:, seq - (w_dw.shape[1] - 1):, :])

        conv_out, new_state = _conv_sample(u_s, state_conv[l], *conv_w)
        h3 = lambda t: t.reshape(db, n_heads, hd)
        sel = _sample_topk(kmean.reshape(db, n_pages // ppb, n_heads, hd), h3(q_s))
        attn = _attn_sample(h3(q_s), h3(k_s), h3(v_s), cache_k, cache_v, l, pt_flat, sel.reshape(-1), n_pages, ppb)
        hs = _post_mixer(hs, conv_out.astype(BF16), attn.astype(BF16), p_sample[l].reshape(db, -1), *post)
        ks_l.append(k_s.reshape(db, 1, n_heads, hd))
        vs_l.append(v_s.reshape(db, 1, n_heads, hd))
        cs_l.append(new_state)

    y_prompt = hp.reshape(batch, seq, d)
    y_sample = hs.reshape(db, 1, d)
    stack = lambda xs: xs[0][None] if len(xs) == 1 else jnp.stack(xs)
    return (y_prompt, y_sample, stack(kp_l), stack(vp_l), stack(cp_l), stack(ks_l), stack(vs_l), stack(cs_l))
```
